```python
import jax
import jax.numpy as jnp
from jax import lax
import numpy as np

D_MODEL = 1024
BATCH = 1
SEQ = 16384
DEPTH = 4

HEAD_DIM = 64
D_FF = 2816
RMS_EPS = 1e-6
Q_BLOCK = 128
NEG_INF = -1e30

LRU_WIDTH = 768
LRU_BLOCKS = 6
LRU_BLOCK_W = LRU_WIDTH // LRU_BLOCKS
CONV_WIDTH = 4
LRU_C = 8.0

NSA_HEADS = 12
NSA_KV_HEADS = 3
NSA_GROUP = NSA_HEADS // NSA_KV_HEADS
NSA_Q_W = NSA_HEADS * HEAD_DIM
NSA_KV_W = NSA_KV_HEADS * HEAD_DIM
CMP_BLOCK = 32
CMP_STRIDE = 16
CMP_HIDDEN = 256
SEL_BLOCK = 64
SEL_TOPN = 16
WIN = 512
FORCE_SCORE = 1e9

DIL_GROUPS = ((128, 1), (512, 4), (2048, 16))
DIL_HEADS_PER_GROUP = 4
DIL_HEADS = DIL_HEADS_PER_GROUP * len(DIL_GROUPS)
DIL_W = DIL_HEADS * HEAD_DIM
DIL_OUT_W = DIL_HEADS_PER_GROUP * HEAD_DIM
DIL_PAD = max(w for w, _ in DIL_GROUPS)

IN_SPLITS = (LRU_WIDTH, LRU_WIDTH,
             NSA_Q_W,
             NSA_KV_W, NSA_KV_W,
             NSA_KV_W, NSA_KV_W,
             NSA_KV_W, NSA_KV_W,
             NSA_HEADS * 3,
             DIL_W, DIL_W, DIL_W,
             D_MODEL, D_MODEL, D_MODEL)
IN_WIDTH = sum(IN_SPLITS)
IN_OFFSETS = tuple(int(o) for o in np.cumsum(IN_SPLITS)[:-1])

kernel_name = 'hybrid_rglru_nsa_dilated_macaron'


def rms_norm(x, g):
    xf = x.astype(jnp.float32)
    y = xf * lax.rsqrt(jnp.mean(xf * xf, axis=-1, keepdims=True) + RMS_EPS)
    return (y * g.astype(jnp.float32)).astype(x.dtype)


def swiglu(x, w1, w3, w2):
    return (jax.nn.silu(x @ w1) * (x @ w3)) @ w2


def masked_softmax(s, mask):
    s = jnp.where(mask, s, NEG_INF)
    m = jnp.max(s, axis=-1, keepdims=True)
    e = jnp.where(mask, jnp.exp(s - m), 0.0)
    den = jnp.maximum(jnp.sum(e, axis=-1, keepdims=True), 1e-30)
    return e / den, m + jnp.log(den)


def causal_conv(x, w, b):
    S_ = x.shape[1]
    xp = jnp.pad(x, ((0, 0), (CONV_WIDTH - 1, 0), (0, 0)))
    out = b
    for j in range(CONV_WIDTH):
        out = out + w[j] * xp[:, j:j + S_]
    return out


def rg_lru(x, w_a, b_a, w_i, b_i, lam):
    B_, S_, _ = x.shape
    xf = x.astype(jnp.float32)
    xb = xf.reshape(B_, S_, LRU_BLOCKS, LRU_BLOCK_W)
    r = jax.nn.sigmoid(jnp.einsum('bshi,hij->bshj', xb, w_a.astype(jnp.float32)).reshape(B_, S_, LRU_WIDTH) + b_a)
    i = jax.nn.sigmoid(jnp.einsum('bshi,hij->bshj', xb, w_i.astype(jnp.float32)).reshape(B_, S_, LRU_WIDTH) + b_i)
    log_a = -LRU_C * r * jax.nn.softplus(-lam.astype(jnp.float32))
    a = jnp.exp(log_a)
    u = jnp.sqrt(-jnp.expm1(2.0 * log_a)) * (i * xf)

    def combine(c1, c2):
        a1, b1 = c1
        a2, b2 = c2
        return a1 * a2, a2 * b1 + b2

    _, h = lax.associative_scan(combine, (a, u), axis=1)
    return h.astype(x.dtype)


def nsa_attention(q, k_cmp, v_cmp, k_slc, v_slc, k_win, v_win, gate_logits,
                  cmp_pos_k, cmp_pos_v, cmp_k_w1, cmp_k_w2, cmp_v_w1, cmp_v_w2):
    B_, S_ = q.shape[0], q.shape[1]
    scale = HEAD_DIM ** -0.5
    q = q.reshape(B_, S_, NSA_KV_HEADS, NSA_GROUP, HEAD_DIM)
    gates = jax.nn.sigmoid(gate_logits.astype(jnp.float32)).reshape(B_, S_, NSA_KV_HEADS, NSA_GROUP, 3)

    def to_heads(t):
        return t.reshape(B_, S_, NSA_KV_HEADS, HEAD_DIM)

    n_cmp = (S_ - CMP_BLOCK) // CMP_STRIDE + 1
    cmp_idx = np.arange(n_cmp)[:, None] * CMP_STRIDE + np.arange(CMP_BLOCK)[None, :]

    def compress(kv, pos, w1, w2):
        blocks = to_heads(kv)[:, cmp_idx] + pos[None, None, :, None, :]
        blocks = jnp.transpose(blocks, (0, 1, 3, 2, 4)).reshape(B_, n_cmp, NSA_KV_HEADS, CMP_BLOCK * HEAD_DIM)
        return jax.nn.gelu(blocks @ w1) @ w2

    kc = compress(k_cmp, cmp_pos_k, cmp_k_w1, cmp_k_w2)
    vc = compress(v_cmp, cmp_pos_v, cmp_v_w1, cmp_v_w2)
    cmp_end = jnp.asarray(np.arange(n_cmp) * CMP_STRIDE + CMP_BLOCK - 1, jnp.int32)

    n_sel = S_ // SEL_BLOCK
    n_top = min(SEL_TOPN, n_sel)
    sel_of_cmp = jax.nn.one_hot(jnp.asarray(np.arange(n_cmp) * CMP_STRIDE // SEL_BLOCK, jnp.int32),
                                n_sel, dtype=jnp.float32)
    sel_blk = jnp.arange(n_sel, dtype=jnp.int32)
    ks_t = jnp.transpose(to_heads(k_slc), (0, 2, 1, 3))
    vs_t = jnp.transpose(to_heads(v_slc), (0, 2, 1, 3))
    gather_rows = jax.vmap(jax.vmap(lambda src, idx: src[idx]))

    kw_p = jnp.pad(to_heads(k_win), ((0, 0), (WIN, 0), (0, 0), (0, 0)))
    vw_p = jnp.pad(to_heads(v_win), ((0, 0), (WIN, 0), (0, 0), (0, 0)))

    def block(qb):
        start = qb * Q_BLOCK
        t = start + jnp.arange(Q_BLOCK, dtype=jnp.int32)
        qq = lax.dynamic_slice_in_dim(q, start, Q_BLOCK, axis=1)
        gg = lax.dynamic_slice_in_dim(gates, start, Q_BLOCK, axis=1)

        s_c = jnp.einsum('bqkgd,bckd->bkgqc', qq, kc).astype(jnp.float32) * scale
        p_c, _ = masked_softmax(s_c, cmp_end[None, :] <= t[:, None])
        o_c = jnp.einsum('bkgqc,bckd->bqkgd', p_c.astype(vc.dtype), vc)

        imp = jnp.einsum('bkgqc,cn->bkqn', p_c, sel_of_cmp)
        cur = (t // SEL_BLOCK)[:, None]
        imp = jnp.where((sel_blk[None, :] == cur) | (sel_blk[None, :] == 0), FORCE_SCORE,
                        jnp.where(sel_blk[None, :] > cur, -1.0, imp))
        _, top = lax.top_k(imp, n_top)
        n_keys = n_top * SEL_BLOCK
        kpos = (top[..., None] * SEL_BLOCK + jnp.arange(SEL_BLOCK, dtype=jnp.int32)).reshape(
            B_, NSA_KV_HEADS, Q_BLOCK * n_keys)
        k_sel = gather_rows(ks_t, kpos).reshape(B_, NSA_KV_HEADS, Q_BLOCK, n_keys, HEAD_DIM)
        v_sel = gather_rows(vs_t, kpos).reshape(B_, NSA_KV_HEADS, Q_BLOCK, n_keys, HEAD_DIM)
        kpos = kpos.reshape(B_, NSA_KV_HEADS, 1, Q_BLOCK, n_keys)
        s_s = jnp.einsum('bqkgd,bkqnd->bkgqn', qq, k_sel).astype(jnp.float32) * scale
        p_s, _ = masked_softmax(s_s, kpos <= t[:, None])
        o_s = jnp.einsum('bkgqn,bkqnd->bqkgd', p_s.astype(v_sel.dtype), v_sel)

        k_w = lax.dynamic_slice_in_dim(kw_p, start, WIN + Q_BLOCK, axis=1)
        v_w = lax.dynamic_slice_in_dim(vw_p, start, WIN + Q_BLOCK, axis=1)
        wpos = start - WIN + jnp.arange(WIN + Q_BLOCK, dtype=jnp.int32)
        dist = t[:, None] - wpos[None, :]
        s_w = jnp.einsum('bqkgd,bskd->bkgqs', qq, k_w).astype(jnp.float32) * scale
        p_w, _ = masked_softmax(s_w, (dist >= 0) & (dist < WIN) & (wpos[None, :] >= 0))
        o_w = jnp.einsum('bkgqs,bskd->bqkgd', p_w.astype(v_w.dtype), v_w)

        o = gg[..., 0:1] * o_c + gg[..., 1:2] * o_s + gg[..., 2:3] * o_w
        return o.reshape(B_, Q_BLOCK, NSA_Q_W).astype(q.dtype)

    out = lax.map(block, jnp.arange(S_ // Q_BLOCK, dtype=jnp.int32))
    return jnp.transpose(out, (1, 0, 2, 3)).reshape(B_, S_, NSA_Q_W)


def dilated_attention(q, k, v):
    B_, S_ = q.shape[0], q.shape[1]
    scale = HEAD_DIM ** -0.5
    n_grp = len(DIL_GROUPS)
    shp = (B_, S_, n_grp, DIL_HEADS_PER_GROUP, HEAD_DIM)
    q = q.reshape(shp)
    pad = ((0, 0), (DIL_PAD, 0), (0, 0), (0, 0), (0, 0))
    k_p = jnp.pad(k.reshape(shp), pad)
    v_p = jnp.pad(v.reshape(shp), pad)

    def block(qb):
        start = qb * Q_BLOCK
        t = start + jnp.arange(Q_BLOCK, dtype=jnp.int32)
        qq = lax.dynamic_slice_in_dim(q, start, Q_BLOCK, axis=1)
        outs, lses = [], []
        for gi, (window, dil) in enumerate(DIL_GROUPS):
            n_keys = window // dil + 1
            kpos = t[:, None] - dil * jnp.arange(n_keys, dtype=jnp.int32)[None, :]
            idx = (kpos + DIL_PAD).reshape(-1)
            kg = jnp.take(k_p[:, :, gi], idx, axis=1).reshape(B_, Q_BLOCK, n_keys, DIL_HEADS_PER_GROUP, HEAD_DIM)
            vg = jnp.take(v_p[:, :, gi], idx, axis=1).reshape(B_, Q_BLOCK, n_keys, DIL_HEADS_PER_GROUP, HEAD_DIM)
            s = jnp.einsum('bqhd,bqnhd->bhqn', qq[:, :, gi], kg).astype(jnp.float32) * scale
            p, lse = masked_softmax(s, kpos >= 0)
            outs.append(jnp.einsum('bhqn,bqnhd->bqhd', p.astype(vg.dtype), vg))
            lses.append(jnp.transpose(lse[..., 0], (0, 2, 1)))
        wts = jax.nn.softmax(jnp.stack(lses, axis=-1), axis=-1)
        o = jnp.sum(jnp.stack(outs, axis=-1) * wts[:, :, :, None, :], axis=-1)
        return o.reshape(B_, Q_BLOCK, DIL_OUT_W).astype(q.dtype)

    out = lax.map(block, jnp.arange(S_ // Q_BLOCK, dtype=jnp.int32))
    return jnp.transpose(out, (1, 0, 2, 3)).reshape(B_, S_, DIL_OUT_W)


def hybrid_mixer(h, w_in, conv_w, conv_b, lru_wa, lru_ba, lru_wi, lru_bi, lru_lambda,
                 cmp_pos_k, cmp_pos_v, cmp_k_w1, cmp_k_w2, cmp_v_w1, cmp_v_w2,
                 w_up_a, w_up_b, w_up_c, w_out):
    (a_x, a_gate, b_q, b_kc, b_vc, b_ks, b_vs, b_kw, b_vw, b_gate,
     c_q, c_k, c_v, m_a, m_b, m_c) = jnp.split(h @ w_in, list(IN_OFFSETS), axis=-1)
    y_a = rg_lru(causal_conv(a_x, conv_w, conv_b), lru_wa, lru_ba, lru_wi, lru_bi, lru_lambda)
    y_a = (y_a * jax.nn.gelu(a_gate)) @ w_up_a
    y_b = nsa_attention(b_q, b_kc, b_vc, b_ks, b_vs, b_kw, b_vw, b_gate,
                        cmp_pos_k, cmp_pos_v, cmp_k_w1, cmp_k_w2, cmp_v_w1, cmp_v_w2) @ w_up_b
    y_c = dilated_attention(c_q, c_k, c_v) @ w_up_c
    merged = jax.nn.sigmoid(m_a) * y_a + jax.nn.sigmoid(m_b) * y_b + jax.nn.sigmoid(m_c) * y_c
    return merged @ w_out


def _normal(key, shape, scale):
    return jax.random.normal(key, shape, jnp.float32) * scale


def setup_inputs(seed: int = 0) -> dict:
    key = jax.random.key(seed)
    k = jax.random.split(key, 32)
    L = DEPTH
    a_lo, a_hi = 0.9 ** (1.0 / LRU_C), 0.999 ** (1.0 / LRU_C)
    a0 = jax.random.uniform(k[13], (L, LRU_WIDTH), jnp.float32, a_lo, a_hi)
    flat = CMP_BLOCK * HEAD_DIM
    return {
        'x': _normal(k[0], (BATCH, SEQ, D_MODEL), 1.0),
        'ffn1_norm': 1.0 + _normal(k[1], (L, D_MODEL), 0.01),
        'ffn1_w1': _normal(k[2], (L, D_MODEL, D_FF), D_MODEL ** -0.5),
        'ffn1_w3': _normal(k[3], (L, D_MODEL, D_FF), D_MODEL ** -0.5),
        'ffn1_w2': _normal(k[4], (L, D_FF, D_MODEL), D_FF ** -0.5),
        'mix_norm': 1.0 + _normal(k[5], (L, D_MODEL), 0.01),
        'w_in': _normal(k[6], (L, D_MODEL, IN_WIDTH), D_MODEL ** -0.5),
        'conv_w': _normal(k[7], (L, CONV_WIDTH, LRU_WIDTH), CONV_WIDTH ** -0.5),
        'conv_b': _normal(k[8], (L, LRU_WIDTH), 0.01),
        'lru_wa': _normal(k[9], (L, LRU_BLOCKS, LRU_BLOCK_W, LRU_BLOCK_W), LRU_BLOCK_W ** -0.5),
        'lru_ba': _normal(k[10], (L, LRU_WIDTH), 0.01),
        'lru_wi': _normal(k[11], (L, LRU_BLOCKS, LRU_BLOCK_W, LRU_BLOCK_W), LRU_BLOCK_W ** -0.5),
        'lru_bi': _normal(k[12], (L, LRU_WIDTH), 0.01),
        'lru_lambda': jnp.log(a0) - jnp.log1p(-a0),
        'cmp_pos_k': _normal(k[14], (L, CMP_BLOCK, HEAD_DIM), 0.1),
        'cmp_pos_v': _normal(k[15], (L, CMP_BLOCK, HEAD_DIM), 0.1),
        'cmp_k_w1': _normal(k[16], (L, flat, CMP_HIDDEN), flat ** -0.5),
        'cmp_k_w2': _normal(k[17], (L, CMP_HIDDEN, HEAD_DIM), CMP_HIDDEN ** -0.5),
        'cmp_v_w1': _normal(k[18], (L, flat, CMP_HIDDEN), flat ** -0.5),
        'cmp_v_w2': _normal(k[19], (L, CMP_HIDDEN, HEAD_DIM), CMP_HIDDEN ** -0.5),
        'w_up_a': _normal(k[20], (L, LRU_WIDTH, D_MODEL), LRU_WIDTH ** -0.5),
        'w_up_b': _normal(k[21], (L, NSA_Q_W, D_MODEL), NSA_Q_W ** -0.5),
        'w_up_c': _normal(k[22], (L, DIL_OUT_W, D_MODEL), DIL_OUT_W ** -0.5),
        'w_out': _normal(k[23], (L, D_MODEL, D_MODEL), D_MODEL ** -0.5),
        'ffn2_norm': 1.0 + _normal(k[24], (L, D_MODEL), 0.01),
        'ffn2_w1': _normal(k[25], (L, D_MODEL, D_FF), D_MODEL ** -0.5),
        'ffn2_w3': _normal(k[26], (L, D_MODEL, D_FF), D_MODEL ** -0.5),
        'ffn2_w2': _normal(k[27], (L, D_FF, D_MODEL), D_FF ** -0.5),
        'final_norm': 1.0 + _normal(k[28], (D_MODEL,), 0.01),
    }


def reference(x, ffn1_norm, ffn1_w1, ffn1_w3, ffn1_w2, mix_norm, w_in, conv_w, conv_b,
              lru_wa, lru_ba, lru_wi, lru_bi, lru_lambda, cmp_pos_k, cmp_pos_v,
              cmp_k_w1, cmp_k_w2, cmp_v_w1, cmp_v_w2, w_up_a, w_up_b, w_up_c, w_out,
              ffn2_norm, ffn2_w1, ffn2_w3, ffn2_w2, final_norm):
    for l in range(DEPTH):
        x = x + 0.5 * swiglu(rms_norm(x, ffn1_norm[l]), ffn1_w1[l], ffn1_w3[l], ffn1_w2[l])
        x = x + hybrid_mixer(rms_norm(x, mix_norm[l]), w_in[l], conv_w[l], conv_b[l],
                             lru_wa[l], lru_ba[l], lru_wi[l], lru_bi[l], lru_lambda[l],
                             cmp_pos_k[l], cmp_pos_v[l], cmp_k_w1[l], cmp_k_w2[l],
                             cmp_v_w1[l], cmp_v_w2[l], w_up_a[l], w_up_b[l], w_up_c[l], w_out[l])
        x = x + 0.5 * swiglu(rms_norm(x, ffn2_norm[l]), ffn2_w1[l], ffn2_w3[l], ffn2_w2[l])
    return rms_norm(x, final_norm)
```

```python
import functools

import jax
import jax.numpy as jnp
import numpy as np
from jax import lax
from jax.experimental import pallas as pl
from jax.experimental.pallas import tpu as pltpu

F32 = jnp.float32
BF16 = jnp.bfloat16

D_MODEL = 1024
D_FF = 2816
HEAD_DIM = 64
RMS_EPS = 1e-6
NEG_INF = -1e30
SCALE = HEAD_DIM ** -0.5

LRU_WIDTH = 768
CONV_WIDTH = 4
LRU_C = 8.0

NSA_HEADS = 12
NSA_KV_HEADS = 3
NSA_GROUP = 4
NSA_Q_W = 768
NSA_KV_W = 192
CMP_BLOCK = 32
CMP_STRIDE = 16
SEL_BLOCK = 64
SEL_TOPN = 16
WIN = 512
FORCE_SCORE = 1e9

DIL_GROUPS = ((128, 1), (512, 4), (2048, 16))
DIL_HEADS_PER_GROUP = 4
DIL_W = 768

IN_SPLITS = (768, 768, 768, 192, 192, 192, 192, 192, 192, 36, 768, 768, 768, 1024, 1024, 1024)
IN_OFFSETS = tuple(int(o) for o in np.cumsum(IN_SPLITS)[:-1])

LANES = 128
V7X_VMEM_LIMIT_BYTES = 56 * 1024 * 1024

FF_CHUNK = 256
FFN_ROWS = 512
PROJ_ROWS = 256
LRU_ROWS = 256
MERGE_ROWS = 256
Q_BLOCK = 128
KEY_CHUNK = 256
GATE_ROWS = 40


def _cparams(sem):
    return pltpu.CompilerParams(dimension_semantics=sem, vmem_limit_bytes=V7X_VMEM_LIMIT_BYTES)


def _resident(shape):
    nd = len(shape)
    return pl.BlockSpec(shape, lambda *_: (0,) * nd, pipeline_mode=pl.Buffered(1))


def _rms(x, g):
    return x * lax.rsqrt(jnp.mean(x * x, axis=-1, keepdims=True) + RMS_EPS) * g


def _gelu_tanh(x):
    c = np.float32(np.sqrt(2.0 / np.pi))
    return x * (0.5 * (1.0 + jnp.tanh(c * (x + 0.044715 * (x * x * x)))))


def _dot(a, b):
    return jnp.dot(a, b, preferred_element_type=F32)


def _dot_nt(a, b):
    return lax.dot_general(a, b, (((1,), (1,)), ((), ())), preferred_element_type=F32)


def _ffn_body(x_ref, g_ref, w1_ref, w3_ref, w2_ref, fg_ref, o_ref, h_ref, acc_ref, *, final):
    x = x_ref[...]
    h_ref[...] = _rms(x, g_ref[...]).astype(BF16)
    acc_ref[...] = jnp.zeros_like(acc_ref)

    def chunk(c, carry):
        h = h_ref[...]
        a = _dot(h, w1_ref[c])
        b = _dot(h, w3_ref[c])
        gated = (a * jax.nn.sigmoid(a) * b).astype(BF16)
        acc_ref[...] += _dot(gated, w2_ref[c])
        return carry

    lax.fori_loop(0, w1_ref.shape[0], chunk, 0)
    y = x + 0.5 * acc_ref[...]
    if final:
        y = _rms(y, fg_ref[...])
    o_ref[...] = y


def _ffn(x, g, w1, w3, w2, fg, *, final):
    s, d = x.shape
    nc = D_FF // FF_CHUNK
    w1c = w1.astype(BF16).reshape(d, nc, FF_CHUNK).transpose(1, 0, 2)
    w3c = w3.astype(BF16).reshape(d, nc, FF_CHUNK).transpose(1, 0, 2)
    w2c = w2.astype(BF16).reshape(nc, FF_CHUNK, d)
    tm = min(FFN_ROWS, s)
    row = pl.BlockSpec((tm, d), lambda i: (i, 0))
    return pl.pallas_call(
        functools.partial(_ffn_body, final=final),
        out_shape=jax.ShapeDtypeStruct((s, d), F32),
        grid=(s // tm,),
        in_specs=[row, _resident((1, d)), _resident(w1c.shape), _resident(w3c.shape),
                  _resident(w2c.shape), _resident((1, d))],
        out_specs=row,
        scratch_shapes=[pltpu.VMEM((tm, d), BF16), pltpu.VMEM((tm, d), F32)],
        compiler_params=_cparams(("parallel",)),
        name="ffn_final" if final else "ffn",
    )(x, g.reshape(1, d), w1c, w3c, w2c, fg.reshape(1, d))


def _proj_body(x_ref, g_ref, w_a_ref, w_qt_ref, w_kvc_ref, w_ksw_ref, w_vswt_ref, w_glt_ref,
               w_cq_ref, w_ckv_ref,
               a2_ref, qt_ref, kvc_ref, ksw_ref, vswt_ref, glt_ref, c_ref):
    h = _rms(x_ref[...], g_ref[...]).astype(BF16)
    a2_ref[...] = _dot(h, w_a_ref[...])
    qt_ref[...] = (_dot_nt(w_qt_ref[...], h) * SCALE).astype(BF16)
    kvc_ref[...] = _dot(h, w_kvc_ref[...])
    ksw_ref[...] = _dot(h, w_ksw_ref[...]).astype(BF16)
    vswt_ref[0] = _dot_nt(w_vswt_ref[...], h).astype(BF16)
    glt_ref[...] = _dot_nt(w_glt_ref[...], h)
    ncq = w_cq_ref.shape[1]
    c_ref[:, :ncq] = (_dot(h, w_cq_ref[...]) * SCALE).astype(BF16)
    c_ref[:, ncq:] = _dot(h, w_ckv_ref[...]).astype(BF16)


def _pad_heads(w, n_heads):
    k = w.shape[0]
    w = w.reshape(k, n_heads, HEAD_DIM)
    w = jnp.pad(w, ((0, 0), (0, 0), (0, LANES - HEAD_DIM)))
    return w.reshape(k, n_heads * LANES)


def _proj(x, g, w_in):
    s, d = x.shape
    (a_x, a_gate, b_q, b_kc, b_vc, b_ks, b_vs, b_kw, b_vw, b_gate,
     c_q, c_k, c_v, _, _, _) = jnp.split(w_in.astype(BF16), list(IN_OFFSETS), axis=1)
    w_a = jnp.concatenate([a_x, a_gate], axis=1)
    w_qt = b_q.T
    w_kvc = jnp.concatenate([b_kc, b_vc], axis=1)
    w_ksw = jnp.concatenate([_pad_heads(b_ks, 3), _pad_heads(b_kw, 3)], axis=1)
    w_vswt = jnp.concatenate([b_vs, b_vw], axis=1).T
    w_glt = jnp.pad(b_gate.T, ((0, GATE_ROWS - b_gate.shape[1]), (0, 0)))
    w_cq = _pad_heads(c_q, 12)
    w_ckv = jnp.concatenate([_pad_heads(c_k, 12), _pad_heads(c_v, 12)], axis=1)
    weights = [w_a, w_qt, w_kvc, w_ksw, w_vswt, w_glt, w_cq, w_ckv]
    tm = PROJ_ROWS
    assert tm == KEY_CHUNK and s % tm == 0

    def rows(n):
        return pl.BlockSpec((tm, n), lambda i: (i, 0))

    def cols(n):
        return pl.BlockSpec((n, tm), lambda i: (0, i))

    out_shape = [
        jax.ShapeDtypeStruct((s, 2 * LRU_WIDTH), F32),
        jax.ShapeDtypeStruct((NSA_Q_W, s), BF16),
        jax.ShapeDtypeStruct((s, 2 * NSA_KV_W), F32),
        jax.ShapeDtypeStruct((s, 6 * LANES), BF16),
        jax.ShapeDtypeStruct((s // tm, 2 * NSA_KV_W, tm), BF16),
        jax.ShapeDtypeStruct((GATE_ROWS, s), F32),
        jax.ShapeDtypeStruct((s, 36 * LANES), BF16),
    ]
    out_specs = [rows(2 * LRU_WIDTH), cols(NSA_Q_W), rows(2 * NSA_KV_W), rows(6 * LANES),
                 pl.BlockSpec((1, 2 * NSA_KV_W, tm), lambda i: (i, 0, 0)), cols(GATE_ROWS),
                 rows(36 * LANES)]
    return pl.pallas_call(
        _proj_body,
        out_shape=out_shape,
        grid=(s // tm,),
        in_specs=[rows(d), _resident((1, d))] + [_resident(w.shape) for w in weights],
        out_specs=out_specs,
        compiler_params=_cparams(("parallel",)),
        name="proj",
    )(x, g.reshape(1, d), *weights)


def _lru_body(a2_ref, cw_ref, cb_ref, wa_ref, ba_ref, wi_ref, bi_ref, lam_ref, y_ref,
              xbuf, a_s, u_s, h_s, hc):
    t_rows = y_ref.shape[0]
    w = LRU_WIDTH

    @pl.when(pl.program_id(0) == 0)
    def _():
        xbuf[0:8, :] = jnp.zeros((8, w), F32)
        hc[...] = jnp.zeros_like(hc)

    x = a2_ref[:, 0:w]
    xbuf[8:8 + t_rows, :] = x
    cw = cw_ref[...]
    xc = cb_ref[...] + cw[0:1] * xbuf[5:5 + t_rows, :]
    xc = xc + cw[1:2] * xbuf[6:6 + t_rows, :]
    xc = xc + cw[2:3] * xbuf[7:7 + t_rows, :]
    xc = xc + cw[3:4] * x
    xbuf[0:8, :] = x[t_rows - 8:t_rows, :]

    xcb = xc.astype(BF16)
    ra, ri = [], []
    for p in range(wa_ref.shape[0]):
        blk = xcb[:, 256 * p:256 * (p + 1)]
        ra.append(_dot(blk, wa_ref[p]))
        ri.append(_dot(blk, wi_ref[p]))
    r = jax.nn.sigmoid(jnp.concatenate(ra, axis=1) + ba_ref[...])
    ig = jax.nn.sigmoid(jnp.concatenate(ri, axis=1) + bi_ref[...])
    z = -lam_ref[...]
    softplus = jnp.maximum(z, 0.0) + jnp.log1p(jnp.exp(-jnp.abs(z)))
    log_a = (-LRU_C * r) * softplus
    a_s[...] = jnp.exp(log_a)
    u_s[...] = jnp.sqrt(1.0 - jnp.exp(2.0 * log_a)) * (ig * xc)

    row = lax.broadcasted_iota(jnp.int32, (8, w), 0)

    def group(gidx, hprev):
        base = pl.multiple_of(gidx * 8, 8)
        a = a_s[pl.ds(base, 8), :]
        b = u_s[pl.ds(base, 8), :]
        for dshift in (1, 2, 4):
            ok = row >= dshift
            a_sh = pltpu.roll(a, dshift, 0)
            b_sh = pltpu.roll(b, dshift, 0)
            b = jnp.where(ok, a * b_sh + b, b)
            a = jnp.where(ok, a * a_sh, a)
        h8 = a * hprev + b
        h_s[pl.ds(base, 8), :] = h8
        return jnp.broadcast_to(h8[7:8, :], (8, w))

    hc[...] = lax.fori_loop(0, t_rows // 8, group, hc[...])
    y_ref[...] = (h_s[...] * _gelu_tanh(a2_ref[:, w:2 * w])).astype(BF16)


def _pair_block_diag(wb):
    z = jnp.zeros((128, 128), wb.dtype)
    return jnp.stack([jnp.block([[wb[2 * p], z], [z, wb[2 * p + 1]]]) for p in range(3)])


def _lru(a2, conv_w, conv_b, wa, ba, wi, bi, lam):
    s = a2.shape[0]
    w = LRU_WIDTH
    t = min(LRU_ROWS, s)
    wa2 = _pair_block_diag(wa.astype(BF16))
    wi2 = _pair_block_diag(wi.astype(BF16))
    vec = _resident((1, w))
    return pl.pallas_call(
        _lru_body,
        out_shape=jax.ShapeDtypeStruct((s, w), BF16),
        grid=(s // t,),
        in_specs=[pl.BlockSpec((t, 2 * w), lambda i: (i, 0)), _resident((CONV_WIDTH, w)), vec,
                  _resident(wa2.shape), vec, _resident(wi2.shape), vec, vec],
        out_specs=pl.BlockSpec((t, w), lambda i: (i, 0)),
        scratch_shapes=[pltpu.VMEM((t + 8, w), F32), pltpu.VMEM((t, w), F32), pltpu.VMEM((t, w), F32),
                        pltpu.VMEM((t, w), F32), pltpu.VMEM((8, w), F32)],
        compiler_params=_cparams(("arbitrary",)),
        name="lru",
    )(a2, conv_w, conv_b.reshape(1, w), wa2, ba.reshape(1, w), wi2, bi.reshape(1, w), lam.reshape(1, w))


CMP_HALF = CMP_STRIDE * NSA_KV_W


def _compress_body(xa_ref, xb_ref, pos_ref, w1_ref, w2_ref, o_ref, *, transposed):
    xa = (xa_ref[...] + pos_ref[:, :CMP_HALF]).astype(BF16)
    xb = (xb_ref[...] + pos_ref[:, CMP_HALF:]).astype(BF16)
    pre = _dot(xa, w1_ref[:CMP_HALF, :]) + _dot(xb, w1_ref[CMP_HALF:, :])
    hid = _gelu_tanh(pre).astype(BF16)
    if transposed:
        o_ref[...] = _dot_nt(w2_ref[...], hid).astype(BF16)
    else:
        o_ref[...] = _dot(hid, w2_ref[...]).astype(BF16)


def _compress(kv, pos, w1, w2, *, transposed):
    s = kv.shape[0]
    n_sel = s // SEL_BLOCK
    rows = kv.reshape(n_sel, SEL_BLOCK * NSA_KV_W)
    nxt = jnp.concatenate([rows[1:, :CMP_HALF], jnp.zeros((1, CMP_HALF), kv.dtype)], axis=0)
    ext = jnp.concatenate([rows, nxt], axis=1)
    eye = jnp.eye(NSA_KV_HEADS, dtype=F32)
    w1e = jnp.einsum("ldn,hg->lhdgn", w1.reshape(CMP_BLOCK, HEAD_DIM, -1), eye)
    w1e = w1e.reshape(CMP_BLOCK * NSA_KV_W, -1).astype(BF16)
    hid = w1.shape[1]
    if transposed:
        w2e = jnp.einsum("nd,hg->gdhn", w2, eye).reshape(NSA_KV_W, NSA_KV_HEADS * hid).astype(BF16)
        out_shape = jax.ShapeDtypeStruct((NSA_KV_W, 4 * n_sel), BF16)
        out_spec = pl.BlockSpec((NSA_KV_W, n_sel), lambda j: (0, j))
    else:
        w2p = jnp.pad(w2, ((0, 0), (0, LANES - HEAD_DIM)))
        w2e = jnp.einsum("nd,hg->hngd", w2p, eye).reshape(NSA_KV_HEADS * hid, NSA_KV_HEADS * LANES).astype(BF16)
        out_shape = jax.ShapeDtypeStruct((4 * n_sel, NSA_KV_HEADS * LANES), BF16)
        out_spec = pl.BlockSpec((n_sel, NSA_KV_HEADS * LANES), lambda j: (j, 0))
    pose = jnp.broadcast_to(pos[:, None, :], (CMP_BLOCK, NSA_KV_HEADS, HEAD_DIM)).reshape(1, -1)
    return pl.pallas_call(
        functools.partial(_compress_body, transposed=transposed),
        out_shape=out_shape,
        grid=(4,),
        in_specs=[pl.BlockSpec((n_sel, CMP_HALF), lambda j: (0, j)),
                  pl.BlockSpec((n_sel, CMP_HALF), lambda j: (0, j + 1)),
                  _resident(pose.shape), _resident(w1e.shape), _resident(w2e.shape)],
        out_specs=out_spec,
        compiler_params=_cparams(("parallel",)),
        name="compress_v" if transposed else "compress_k",
    )(ext, ext, pose, w1e, w2e)


def _cmp_topk_body(qt_ref, kc_ref, vct_ref, oct_ref, sel_ref, *, n_sel, n_top):
    i = pl.program_id(0)
    n_cmp = 4 * n_sel
    lane_t = i * Q_BLOCK + lax.broadcasted_iota(jnp.int32, (1, Q_BLOCK), 1)
    r = lax.broadcasted_iota(jnp.int32, (n_cmp, Q_BLOCK), 0)
    cmp_end = SEL_BLOCK * (r % n_sel) + CMP_STRIDE * (r // n_sel) + (CMP_BLOCK - 1)
    valid = cmp_end <= lane_t
    blk = lax.broadcasted_iota(jnp.int32, (n_sel, Q_BLOCK), 0)
    blk_f = blk.astype(F32)
    cur = lane_t // SEL_BLOCK
    forced = (blk == cur) | (blk == 0)
    future = blk > cur

    for k in range(NSA_KV_HEADS):
        kc = kc_ref[:, LANES * k:LANES * k + HEAD_DIM]
        vct = vct_ref[HEAD_DIM * k:HEAD_DIM * (k + 1), :]
        imp = jnp.zeros((n_sel, Q_BLOCK), F32)
        for g in range(NSA_GROUP):
            h = NSA_GROUP * k + g
            q = qt_ref[HEAD_DIM * h:HEAD_DIM * (h + 1), :]
            s = jnp.where(valid, _dot(kc, q), NEG_INF)
            m = jnp.max(s, axis=0, keepdims=True)
            e = jnp.where(valid, jnp.exp(s - m), 0.0)
            den = jnp.maximum(jnp.sum(e, axis=0, keepdims=True), 1e-30)
            p = e * (1.0 / den)
            oct_ref[HEAD_DIM * h:HEAD_DIM * (h + 1), :] = _dot(vct, p.astype(BF16))
            for j in range(4):
                imp = imp + p[j * n_sel:(j + 1) * n_sel, :]
        score = jnp.where(forced, FORCE_SCORE, jnp.where(future, -1.0, imp))

        def pick_one(_, carry):
            v, chosen = carry
            top = jnp.max(v, axis=0, keepdims=True)
            first = jnp.min(jnp.where(v == top, blk_f, float(n_sel)), axis=0, keepdims=True)
            hit = blk_f == first
            return jnp.where(hit, -jnp.inf, v), jnp.where(hit, 1.0, chosen)

        _, chosen = lax.fori_loop(0, n_top, pick_one, (score, jnp.zeros((n_sel, Q_BLOCK), F32)))
        sel_ref[k] = chosen


def _cmp_topk(qt, kc, vct):
    s = qt.shape[1]
    n_sel = s // SEL_BLOCK
    n_top = min(SEL_TOPN, n_sel)
    qspec = pl.BlockSpec((NSA_Q_W, Q_BLOCK), lambda i: (0, i))
    return pl.pallas_call(
        functools.partial(_cmp_topk_body, n_sel=n_sel, n_top=n_top),
        out_shape=[jax.ShapeDtypeStruct((NSA_Q_W, s), F32),
                   jax.ShapeDtypeStruct((NSA_KV_HEADS, n_sel, s), F32)],
        grid=(s // Q_BLOCK,),
        in_specs=[qspec, _resident(kc.shape), _resident(vct.shape)],
        out_specs=[qspec, pl.BlockSpec((NSA_KV_HEADS, n_sel, Q_BLOCK), lambda i: (0, 0, i))],
        compiler_params=_cparams(("parallel",)),
        name="cmp_topk",
    )(qt, kc, vct)


def _band_body(*refs, mode, k_off, v_off):
    if mode == "sel":
        qt_ref, k_ref, vt_ref, sel_ref, o_ref = refs
    else:
        qt_ref, k_ref, vt_ref, o_ref = refs
        sel_ref = None
    i = pl.program_id(0)
    wide = NSA_GROUP * Q_BLOCK
    lane = lax.broadcasted_iota(jnp.int32, (1, wide), 1)
    t = i * Q_BLOCK + lane % Q_BLOCK
    krow = lax.broadcasted_iota(jnp.int32, (KEY_CHUNK, wide), 0)
    t_last = i * Q_BLOCK + (Q_BLOCK - 1)
    c_hi = t_last // KEY_CHUNK + 1
    if mode == "sel":
        c_lo = 0
    else:
        c_lo = jnp.maximum(i * Q_BLOCK - (WIN - 1), 0) // KEY_CHUNK

    for k in range(NSA_KV_HEADS):
        q4 = jnp.concatenate(
            [qt_ref[HEAD_DIM * (NSA_GROUP * k + g):HEAD_DIM * (NSA_GROUP * k + g + 1), :]
             for g in range(NSA_GROUP)], axis=1)

        def chunk(c, carry, k=k, q4=q4):
            m, l, acc = carry
            base = pl.multiple_of(c * KEY_CHUNK, KEY_CHUNK)
            keys = k_ref[pl.ds(base, KEY_CHUNK), k_off + LANES * k:k_off + LANES * k + HEAD_DIM]
            s = _dot(keys, q4)
            kpos = base + krow
            if mode == "sel":
                flags = []
                for rblk in range(KEY_CHUNK // SEL_BLOCK):
                    f = sel_ref[k, pl.ds(c * (KEY_CHUNK // SEL_BLOCK) + rblk, 1), :]
                    f = jnp.concatenate([f] * NSA_GROUP, axis=1)
                    flags.append(jnp.broadcast_to(f, (SEL_BLOCK, wide)))
                mask = (jnp.concatenate(flags, axis=0) > 0.5) & (kpos <= t)
            else:
                dist = t - kpos
                mask = (dist >= 0) & (dist < WIN)
            s = jnp.where(mask, s, NEG_INF)
            m_new = jnp.maximum(m, jnp.max(s, axis=0, keepdims=True))
            alpha = jnp.exp(m - m_new)
            e = jnp.where(mask, jnp.exp(s - m_new), 0.0)
            l_new = alpha * l + jnp.sum(e, axis=0, keepdims=True)
            vt = vt_ref[c, v_off + HEAD_DIM * k:v_off + HEAD_DIM * (k + 1), :]
            acc_new = alpha * acc + _dot(vt, e.astype(BF16))
            return m_new, l_new, acc_new

        init = (jnp.full((1, wide), NEG_INF, F32), jnp.zeros((1, wide), F32),
                jnp.zeros((HEAD_DIM, wide), F32))
        _, l, acc = lax.fori_loop(c_lo, c_hi, chunk, init)
        out = acc * (1.0 / jnp.maximum(l, 1e-30))
        for g in range(NSA_GROUP):
            h = NSA_GROUP * k + g
            o_ref[HEAD_DIM * h:HEAD_DIM * (h + 1), :] = out[:, Q_BLOCK * g:Q_BLOCK * (g + 1)]


def _band(qt, ksw, vswt, sel, *, mode):
    s = qt.shape[1]
    qspec = pl.BlockSpec((NSA_Q_W, Q_BLOCK), lambda i: (0, i))
    k_off = 0 if mode == "sel" else NSA_KV_HEADS * LANES
    v_off = 0 if mode == "sel" else NSA_KV_W
    in_specs = [qspec, _resident(ksw.shape), _resident(vswt.shape)]
    args = [qt, ksw, vswt]
    if mode == "sel":
        in_specs.append(pl.BlockSpec((NSA_KV_HEADS, sel.shape[1], Q_BLOCK), lambda i: (0, 0, i)))
        args.append(sel)
    return pl.pallas_call(
        functools.partial(_band_body, mode=mode, k_off=k_off, v_off=v_off),
        out_shape=jax.ShapeDtypeStruct((NSA_Q_W, s), F32),
        grid=(s // Q_BLOCK,),
        in_specs=in_specs,
        out_specs=qspec,
        compiler_params=_cparams(("parallel",)),
        name="nsa_" + mode,
    )(*args)


DIL_BLOCK_W = DIL_HEADS_PER_GROUP * LANES
DIL_ROW_W = 36 * LANES


def _dil_body(q_ref, kp_ref, kc_ref, vp_ref, vc_ref, o_ref, lse_ref, *, span):
    i = pl.program_id(1)
    rq = lax.broadcasted_iota(jnp.int32, (Q_BLOCK, 2 * Q_BLOCK), 0)
    ck = lax.broadcasted_iota(jnp.int32, (Q_BLOCK, 2 * Q_BLOCK), 1)
    delta = Q_BLOCK + rq - ck
    key_idx = (i - 1) * Q_BLOCK + ck
    valid = (delta >= 0) & (delta <= span) & (key_idx >= 0)
    for j in range(DIL_HEADS_PER_GROUP):
        q = q_ref[:, LANES * j:LANES * j + HEAD_DIM]
        kk = jnp.concatenate([kp_ref[:, LANES * j:LANES * j + HEAD_DIM],
                              kc_ref[:, LANES * j:LANES * j + HEAD_DIM]], axis=0)
        vv = jnp.concatenate([vp_ref[:, LANES * j:LANES * (j + 1)],
                              vc_ref[:, LANES * j:LANES * (j + 1)]], axis=0)
        s = jnp.where(valid, _dot_nt(q, kk), NEG_INF)
        m = jnp.max(s, axis=1, keepdims=True)
        e = jnp.where(valid, jnp.exp(s - m), 0.0)
        den = jnp.maximum(jnp.sum(e, axis=1, keepdims=True), 1e-30)
        p = e * (1.0 / den)
        o_ref[:, LANES * j:LANES * (j + 1)] = _dot(p.astype(BF16), vv)
        lse_ref[:, LANES * j:LANES * (j + 1)] = jnp.broadcast_to(m + jnp.log(den), (Q_BLOCK, LANES))


def _dilated_group(c, gi):
    s = c.shape[0]
    window, dil = DIL_GROUPS[gi]
    span = window // dil
    assert span == Q_BLOCK
    rows = s // dil
    view = c.reshape(rows, dil * DIL_ROW_W)
    per_row = DIL_ROW_W // DIL_BLOCK_W
    spec = lambda tensor, prev: pl.BlockSpec(
        (Q_BLOCK, DIL_BLOCK_W),
        lambda r, i: (jnp.maximum(i - 1, 0) if prev else i, per_row * r + 3 * tensor + gi))
    ospec = pl.BlockSpec((Q_BLOCK, DIL_BLOCK_W), lambda r, i: (i, r))
    o, lse = pl.pallas_call(
        functools.partial(_dil_body, span=span),
        out_shape=[jax.ShapeDtypeStruct((rows, dil * DIL_BLOCK_W), F32)] * 2,
        grid=(dil, rows // Q_BLOCK),
        in_specs=[spec(0, False), spec(1, True), spec(1, False), spec(2, True), spec(2, False)],
        out_specs=[ospec, ospec],
        compiler_params=_cparams(("parallel", "parallel")),
        name="dilated_%d" % gi,
    )(view, view, view, view, view)
    return o.reshape(s, DIL_BLOCK_W), lse.reshape(s, DIL_BLOCK_W)


def _merge_body(x_ref, g_ref, ya_ref, oct_ref, ost_ref, owt_ref, glt_ref,
                o0_ref, l0_ref, o1_ref, l1_ref, o2_ref, l2_ref,
                wm_ref, wua_ref, wub_ref, wuc_ref, wo_ref, out_ref):
    x = x_ref[...]
    h = _rms(x, g_ref[...]).astype(BF16)
    gates = jax.nn.sigmoid(_dot(h, wm_ref[...]))
    y_a = _dot(ya_ref[...], wua_ref[...])

    bg = jax.nn.sigmoid(glt_ref[...])
    pieces = []
    for hd in range(NSA_HEADS):
        rows = slice(HEAD_DIM * hd, HEAD_DIM * (hd + 1))
        pieces.append(bg[3 * hd:3 * hd + 1, :] * oct_ref[rows, :]
                      + bg[3 * hd + 1:3 * hd + 2, :] * ost_ref[rows, :]
                      + bg[3 * hd + 2:3 * hd + 3, :] * owt_ref[rows, :])
    yb_t = jnp.concatenate(pieces, axis=0)
    y_b = _dot(yb_t.T.astype(BF16), wub_ref[...])

    l0, l1, l2 = l0_ref[...], l1_ref[...], l2_ref[...]
    mx = jnp.maximum(jnp.maximum(l0, l1), l2)
    e0, e1, e2 = jnp.exp(l0 - mx), jnp.exp(l1 - mx), jnp.exp(l2 - mx)
    inv = 1.0 / (e0 + e1 + e2)
    yc = (e0 * inv) * o0_ref[...] + (e1 * inv) * o1_ref[...] + (e2 * inv) * o2_ref[...]
    y_c = _dot(yc.astype(BF16), wuc_ref[...])

    d = x.shape[1]
    merged = gates[:, 0:d] * y_a + gates[:, d:2 * d] * y_b + gates[:, 2 * d:3 * d] * y_c
    out_ref[...] = x + _dot(merged.astype(BF16), wo_ref[...])


def _merge(x, g, ya, oct_, ost, owt, glt, dil, w_in, w_up_a, w_up_b, w_up_c, w_out):
    s, d = x.shape
    tm = min(MERGE_ROWS, s)
    w_m = w_in[:, IN_OFFSETS[12]:].astype(BF16)
    wuc = w_up_c.reshape(DIL_HEADS_PER_GROUP, HEAD_DIM, d)
    wuc = jnp.pad(wuc, ((0, 0), (0, LANES - HEAD_DIM), (0, 0))).reshape(DIL_BLOCK_W, d).astype(BF16)
    weights = [w_m, w_up_a.astype(BF16), w_up_b.astype(BF16), wuc, w_out.astype(BF16)]

    def rows(n):
        return pl.BlockSpec((tm, n), lambda i: (i, 0))

    def cols(n):
        return pl.BlockSpec((n, tm), lambda i: (0, i))

    dil_args, dil_specs = [], []
    for o, lse in dil:
        dil_args += [o, lse]
        dil_specs += [rows(DIL_BLOCK_W), rows(DIL_BLOCK_W)]
    return pl.pallas_call(
        _merge_body,
        out_shape=jax.ShapeDtypeStruct((s, d), F32),
        grid=(s // tm,),
        in_specs=[rows(d), _resident((1, d)), rows(LRU_WIDTH), cols(NSA_Q_W), cols(NSA_Q_W),
                  cols(NSA_Q_W), cols(GATE_ROWS)] + dil_specs + [_resident(w.shape) for w in weights],
        out_specs=rows(d),
        compiler_params=_cparams(("parallel",)),
        name="merge",
    )(x, g.reshape(1, d), ya, oct_, ost, owt, glt, *dil_args, *weights)


def _mixer(x, mix_norm, w_in, conv_w, conv_b, lru_wa, lru_ba, lru_wi, lru_bi, lru_lambda,
           cmp_pos_k, cmp_pos_v, cmp_k_w1, cmp_k_w2, cmp_v_w1, cmp_v_w2,
           w_up_a, w_up_b, w_up_c, w_out):
    a2, qt, kvc, ksw, vswt, glt, c = _proj(x, mix_norm, w_in)
    ya = _lru(a2, conv_w, conv_b, lru_wa, lru_ba, lru_wi, lru_bi, lru_lambda)
    kc = _compress(kvc[:, :NSA_KV_W], cmp_pos_k, cmp_k_w1, cmp_k_w2, transposed=False)
    vct = _compress(kvc[:, NSA_KV_W:], cmp_pos_v, cmp_v_w1, cmp_v_w2, transposed=True)
    oct_, sel = _cmp_topk(qt, kc, vct)
    ost = _band(qt, ksw, vswt, sel, mode="sel")
    owt = _band(qt, ksw, vswt, None, mode="win")
    dil = [_dilated_group(c, gi) for gi in range(len(DIL_GROUPS))]
    return _merge(x, mix_norm, ya, oct_, ost, owt, glt, dil, w_in, w_up_a, w_up_b, w_up_c, w_out)


def kernel(x, ffn1_norm, ffn1_w1, ffn1_w3, ffn1_w2, mix_norm, w_in, conv_w, conv_b, lru_wa, lru_ba, lru_wi, lru_bi, lru_lambda, cmp_pos_k, cmp_pos_v, cmp_k_w1, cmp_k_w2, cmp_v_w1, cmp_v_w2, w_up_a, w_up_b, w_up_c, w_out, ffn2_norm, ffn2_w1, ffn2_w3, ffn2_w2, final_norm):
    batch, s, d = x.shape
    assert batch == 1 and d == D_MODEL
    depth = w_in.shape[0]
    y = x.reshape(s, d)
    for l in range(depth):
        y = _ffn(y, ffn1_norm[l], ffn1_w1[l], ffn1_w3[l], ffn1_w2[l], final_norm, final=False)
        y = _mixer(y, mix_norm[l], w_in[l], conv_w[l], conv_b[l], lru_wa[l], lru_ba[l], lru_wi[l],
                   lru_bi[l], lru_lambda[l], cmp_pos_k[l], cmp_pos_v[l], cmp_k_w1[l], cmp_k_w2[l],
                   cmp_v_w1[l], cmp_v_w2[l], w_up_a[l], w_up_b[l], w_up_c[l], w_out[l])
        y = _ffn(y, ffn2_norm[l], ffn2_w1[l], ffn2_w3[l], ffn2_w2[l], final_norm,
                 final=(l == depth - 1))
    return y.reshape(batch, s, d)
```

```python
import functools

import jax
import jax.numpy as jnp
import numpy as np
from jax import lax
from jax.experimental import pallas as pl
from jax.experimental.pallas import tpu as pltpu

F32 = jnp.float32
BF16 = jnp.bfloat16

D_MODEL = 1024
D_FF = 2816
HEAD_DIM = 64
RMS_EPS = 1e-6
NEG_INF = -1e30
SCALE = HEAD_DIM ** -0.5
QK_SCALE_LOG2 = float(SCALE * np.log2(np.e))

LRU_WIDTH = 768
CONV_WIDTH = 4
LRU_C = 8.0

NSA_HEADS = 12
NSA_KV_HEADS = 3
NSA_GROUP = 4
NSA_Q_W = 768
NSA_KV_W = 192
CMP_BLOCK = 32
CMP_STRIDE = 16
SEL_BLOCK = 64
SEL_TOPN = 16
WIN = 512
FORCE_SCORE = 1e9

DIL_GROUPS = ((128, 1), (512, 4), (2048, 16))
DIL_HEADS_PER_GROUP = 4
DIL_W = 768

IN_SPLITS = (768, 768, 768, 192, 192, 192, 192, 192, 192, 36, 768, 768, 768, 1024, 1024, 1024)
IN_OFFSETS = tuple(int(o) for o in np.cumsum(IN_SPLITS)[:-1])

LANES = 128
V7X_VMEM_LIMIT_BYTES = 56 * 1024 * 1024

FF_CHUNK = 256
FFN_ROWS = 512
PROJ_ROWS = 256
LRU_ROWS = 256
MERGE_ROWS = 256
Q_BLOCK = 128
KEY_CHUNK = 256
BAND_Q = 256
GATE_ROWS = 40
CMP_TILE = 64
DIL_BLOCK_W = DIL_HEADS_PER_GROUP * LANES
DIL_GROUP_W = 3 * DIL_BLOCK_W


def _cparams(sem):
    return pltpu.CompilerParams(dimension_semantics=sem, vmem_limit_bytes=V7X_VMEM_LIMIT_BYTES)


def _resident(shape):
    nd = len(shape)
    return pl.BlockSpec(shape, lambda *_: (0,) * nd, pipeline_mode=pl.Buffered(1))


def _rms(x, g):
    return x * lax.rsqrt(jnp.mean(x * x, axis=-1, keepdims=True) + RMS_EPS) * g


def _gelu_tanh(x):
    c = np.float32(np.sqrt(2.0 / np.pi))
    return x * (0.5 * (1.0 + jnp.tanh(c * (x + 0.044715 * (x * x * x)))))


def _dot(a, b):
    return jnp.dot(a, b, preferred_element_type=F32)


def _dot_nt(a, b):
    return lax.dot_general(a, b, (((1,), (1,)), ((), ())), preferred_element_type=F32)


def _ffn_body(x_ref, g_ref, w1_ref, w3_ref, w2_ref, fg_ref, o_ref, h_ref, acc_ref, *, final):
    x = x_ref[...]
    h_ref[...] = _rms(x, g_ref[...]).astype(BF16)
    acc_ref[...] = jnp.zeros_like(acc_ref)

    def chunk(c, carry):
        h = h_ref[...]
        a = _dot(h, w1_ref[c])
        b = _dot(h, w3_ref[c])
        gated = (a * jax.nn.sigmoid(a) * b).astype(BF16)
        acc_ref[...] += _dot(gated, w2_ref[c])
        return carry

    lax.fori_loop(0, w1_ref.shape[0], chunk, 0)
    y = x + 0.5 * acc_ref[...]
    if final:
        y = _rms(y, fg_ref[...])
    o_ref[...] = y


def _ffn(x, g, w1, w3, w2, fg, *, final):
    s, d = x.shape
    nc = D_FF // FF_CHUNK
    w1c = w1.astype(BF16).reshape(d, nc, FF_CHUNK).transpose(1, 0, 2)
    w3c = w3.astype(BF16).reshape(d, nc, FF_CHUNK).transpose(1, 0, 2)
    w2c = w2.astype(BF16).reshape(nc, FF_CHUNK, d)
    tm = min(FFN_ROWS, s)
    row = pl.BlockSpec((tm, d), lambda i: (i, 0))
    return pl.pallas_call(
        functools.partial(_ffn_body, final=final),
        out_shape=jax.ShapeDtypeStruct((s, d), F32),
        grid=(s // tm,),
        in_specs=[row, _resident((1, d)), _resident(w1c.shape), _resident(w3c.shape),
                  _resident(w2c.shape), _resident((1, d))],
        out_specs=row,
        scratch_shapes=[pltpu.VMEM((tm, d), BF16), pltpu.VMEM((tm, d), F32)],
        compiler_params=_cparams(("parallel",)),
        name="ffn_final" if final else "ffn",
    )(x, g.reshape(1, d), w1c, w3c, w2c, fg.reshape(1, d))


def _proj_body(x_ref, g_ref, w_a_ref, w_qt_ref, w_kvc_ref, w_ksw_ref, w_vswt_ref, w_glt_ref,
               w_c0_ref, w_c1_ref, w_c2_ref,
               a2_ref, qt_ref, kvc_ref, ksw_ref, vswt_ref, glt_ref, c0_ref, c1_ref, c2_ref):
    h = _rms(x_ref[...], g_ref[...]).astype(BF16)
    a2_ref[...] = _dot(h, w_a_ref[...])
    qt_ref[...] = (_dot_nt(w_qt_ref[...], h) * QK_SCALE_LOG2).astype(BF16)
    kvc_ref[...] = _dot(h, w_kvc_ref[...])
    ksw_ref[...] = _dot(h, w_ksw_ref[...]).astype(BF16)
    vswt_ref[0] = _dot_nt(w_vswt_ref[...], h).astype(BF16)
    glt_ref[...] = _dot_nt(w_glt_ref[...], h)
    for w_ref, c_ref in ((w_c0_ref, c0_ref), (w_c1_ref, c1_ref), (w_c2_ref, c2_ref)):
        c_ref[:, :DIL_BLOCK_W] = (_dot(h, w_ref[:, :DIL_BLOCK_W]) * SCALE).astype(BF16)
        c_ref[:, DIL_BLOCK_W:] = _dot(h, w_ref[:, DIL_BLOCK_W:]).astype(BF16)


def _pad_heads(w, n_heads):
    k = w.shape[0]
    w = w.reshape(k, n_heads, HEAD_DIM)
    w = jnp.pad(w, ((0, 0), (0, 0), (0, LANES - HEAD_DIM)))
    return w.reshape(k, n_heads * LANES)


def _proj(x, g, w_in):
    s, d = x.shape
    (a_x, a_gate, b_q, b_kc, b_vc, b_ks, b_vs, b_kw, b_vw, b_gate,
     c_q, c_k, c_v, _, _, _) = jnp.split(w_in.astype(BF16), list(IN_OFFSETS), axis=1)
    w_a = jnp.concatenate([a_x, a_gate], axis=1)
    w_qt = b_q.T
    w_kvc = jnp.concatenate([b_kc, b_vc], axis=1)
    w_ksw = jnp.concatenate([_pad_heads(b_ks, 3), _pad_heads(b_kw, 3)], axis=1)
    w_vswt = jnp.concatenate([b_vs, b_vw], axis=1).T
    w_glt = jnp.pad(b_gate.T, ((0, GATE_ROWS - b_gate.shape[1]), (0, 0)))
    gw = DIL_HEADS_PER_GROUP * HEAD_DIM
    w_c = [jnp.concatenate([_pad_heads(t[:, gi * gw:(gi + 1) * gw], DIL_HEADS_PER_GROUP)
                            for t in (c_q, c_k, c_v)], axis=1) for gi in range(len(DIL_GROUPS))]
    weights = [w_a, w_qt, w_kvc, w_ksw, w_vswt, w_glt] + w_c
    tm = PROJ_ROWS
    assert tm == KEY_CHUNK and s % tm == 0

    def rows(n):
        return pl.BlockSpec((tm, n), lambda i: (i, 0))

    def cols(n):
        return pl.BlockSpec((n, tm), lambda i: (0, i))

    out_shape = [
        jax.ShapeDtypeStruct((s, 2 * LRU_WIDTH), F32),
        jax.ShapeDtypeStruct((NSA_Q_W, s), BF16),
        jax.ShapeDtypeStruct((s, 2 * NSA_KV_W), F32),
        jax.ShapeDtypeStruct((s, 6 * LANES), BF16),
        jax.ShapeDtypeStruct((s // tm, 2 * NSA_KV_W, tm), BF16),
        jax.ShapeDtypeStruct((GATE_ROWS, s), F32),
    ] + [jax.ShapeDtypeStruct((s, DIL_GROUP_W), BF16)] * len(DIL_GROUPS)
    out_specs = [rows(2 * LRU_WIDTH), cols(NSA_Q_W), rows(2 * NSA_KV_W), rows(6 * LANES),
                 pl.BlockSpec((1, 2 * NSA_KV_W, tm), lambda i: (i, 0, 0)), cols(GATE_ROWS)
                 ] + [rows(DIL_GROUP_W)] * len(DIL_GROUPS)
    return pl.pallas_call(
        _proj_body,
        out_shape=out_shape,
        grid=(s // tm,),
        in_specs=[rows(d), _resident((1, d))] + [_resident(w.shape) for w in weights],
        out_specs=out_specs,
        compiler_params=_cparams(("parallel",)),
        name="proj",
    )(x, g.reshape(1, d), *weights)


def _lru_body(a2_ref, cw_ref, cb_ref, wa_ref, ba_ref, wi_ref, bi_ref, lam_ref, y_ref,
              xbuf, a_s, u_s, h_s, hc):
    t_rows = y_ref.shape[0]
    w = LRU_WIDTH

    @pl.when(pl.program_id(0) == 0)
    def _():
        xbuf[0:8, :] = jnp.zeros((8, w), F32)
        hc[...] = jnp.zeros_like(hc)

    x = a2_ref[:, 0:w]
    xbuf[8:8 + t_rows, :] = x
    cw = cw_ref[...]
    xc = cb_ref[...] + cw[0:1] * xbuf[5:5 + t_rows, :]
    xc = xc + cw[1:2] * xbuf[6:6 + t_rows, :]
    xc = xc + cw[2:3] * xbuf[7:7 + t_rows, :]
    xc = xc + cw[3:4] * x
    xbuf[0:8, :] = x[t_rows - 8:t_rows, :]

    xcb = xc.astype(BF16)
    ra, ri = [], []
    for p in range(wa_ref.shape[0]):
        blk = xcb[:, 256 * p:256 * (p + 1)]
        ra.append(_dot(blk, wa_ref[p]))
        ri.append(_dot(blk, wi_ref[p]))
    r = jax.nn.sigmoid(jnp.concatenate(ra, axis=1) + ba_ref[...])
    ig = jax.nn.sigmoid(jnp.concatenate(ri, axis=1) + bi_ref[...])
    z = -lam_ref[...]
    softplus = jnp.maximum(z, 0.0) + jnp.log1p(jnp.exp(-jnp.abs(z)))
    log_a = (-LRU_C * r) * softplus
    a_s[...] = jnp.exp(log_a)
    u_s[...] = jnp.sqrt(1.0 - jnp.exp(2.0 * log_a)) * (ig * xc)

    row = lax.broadcasted_iota(jnp.int32, (8, w), 0)

    def group(gidx, hprev):
        base = pl.multiple_of(gidx * 8, 8)
        a = a_s[pl.ds(base, 8), :]
        b = u_s[pl.ds(base, 8), :]
        for dshift in (1, 2, 4):
            ok = row >= dshift
            a_sh = pltpu.roll(a, dshift, 0)
            b_sh = pltpu.roll(b, dshift, 0)
            b = jnp.where(ok, a * b_sh + b, b)
            a = jnp.where(ok, a * a_sh, a)
        h8 = a * hprev + b
        h_s[pl.ds(base, 8), :] = h8
        return jnp.broadcast_to(h8[7:8, :], (8, w))

    hc[...] = lax.fori_loop(0, t_rows // 8, group, hc[...])
    y_ref[...] = (h_s[...] * _gelu_tanh(a2_ref[:, w:2 * w])).astype(BF16)


def _pair_block_diag(wb):
    z = jnp.zeros((128, 128), wb.dtype)
    return jnp.stack([jnp.block([[wb[2 * p], z], [z, wb[2 * p + 1]]]) for p in range(3)])


def _lru(a2, conv_w, conv_b, wa, ba, wi, bi, lam):
    s = a2.shape[0]
    w = LRU_WIDTH
    t = min(LRU_ROWS, s)
    wa2 = _pair_block_diag(wa.astype(BF16))
    wi2 = _pair_block_diag(wi.astype(BF16))
    vec = _resident((1, w))
    return pl.pallas_call(
        _lru_body,
        out_shape=jax.ShapeDtypeStruct((s, w), BF16),
        grid=(s // t,),
        in_specs=[pl.BlockSpec((t, 2 * w), lambda i: (i, 0)), _resident((CONV_WIDTH, w)), vec,
                  _resident(wa2.shape), vec, _resident(wi2.shape), vec, vec],
        out_specs=pl.BlockSpec((t, w), lambda i: (i, 0)),
        scratch_shapes=[pltpu.VMEM((t + 8, w), F32), pltpu.VMEM((t, w), F32), pltpu.VMEM((t, w), F32),
                        pltpu.VMEM((t, w), F32), pltpu.VMEM((8, w), F32)],
        compiler_params=_cparams(("arbitrary",)),
        name="lru",
    )(a2, conv_w, conv_b.reshape(1, w), wa2, ba.reshape(1, w), wi2, bi.reshape(1, w), lam.reshape(1, w))


CMP_HALF = CMP_STRIDE * NSA_KV_W


def _compress_body(x_ref, pos_ref, w1_ref, w2_ref, o_ref, *, transposed):
    hids = []
    for j in range(4):
        xa = (x_ref[:, CMP_HALF * j:CMP_HALF * (j + 1)] + pos_ref[:, :CMP_HALF]).astype(BF16)
        xb = (x_ref[:, CMP_HALF * (j + 1):CMP_HALF * (j + 2)] + pos_ref[:, CMP_HALF:]).astype(BF16)
        pre = _dot(xa, w1_ref[:CMP_HALF, :]) + _dot(xb, w1_ref[CMP_HALF:, :])
        hids.append(_gelu_tanh(pre).astype(BF16))
    hid = jnp.concatenate(hids, axis=0)
    if transposed:
        o_ref[...] = _dot_nt(w2_ref[...], hid).astype(BF16)
    else:
        o_ref[...] = _dot(hid, w2_ref[...]).astype(BF16)


def _compress(kv, pos, w1, w2, *, transposed):
    s = kv.shape[0]
    n_sel = s // SEL_BLOCK
    assert n_sel % CMP_TILE == 0
    rows = kv.reshape(n_sel, SEL_BLOCK * NSA_KV_W)
    nxt = jnp.concatenate([rows[1:, :CMP_HALF], jnp.zeros((1, CMP_HALF), kv.dtype)], axis=0)
    ext = jnp.concatenate([rows, nxt], axis=1)
    eye = jnp.eye(NSA_KV_HEADS, dtype=F32)
    w1e = jnp.einsum("ldn,hg->lhdgn", w1.reshape(CMP_BLOCK, HEAD_DIM, -1), eye)
    w1e = w1e.reshape(CMP_BLOCK * NSA_KV_W, -1).astype(BF16)
    hid = w1.shape[1]
    if transposed:
        w2e = jnp.einsum("nd,hg->gdhn", w2, eye).reshape(NSA_KV_W, NSA_KV_HEADS * hid).astype(BF16)
        out_shape = jax.ShapeDtypeStruct((NSA_KV_W, 4 * n_sel), BF16)
        out_spec = pl.BlockSpec((NSA_KV_W, 4 * CMP_TILE), lambda tt: (0, tt))
    else:
        w2p = jnp.pad(w2, ((0, 0), (0, LANES - HEAD_DIM)))
        w2e = jnp.einsum("nd,hg->hngd", w2p, eye).reshape(NSA_KV_HEADS * hid, NSA_KV_HEADS * LANES).astype(BF16)
        out_shape = jax.ShapeDtypeStruct((4 * n_sel, NSA_KV_HEADS * LANES), BF16)
        out_spec = pl.BlockSpec((4 * CMP_TILE, NSA_KV_HEADS * LANES), lambda tt: (tt, 0))
    pose = jnp.broadcast_to(pos[:, None, :], (CMP_BLOCK, NSA_KV_HEADS, HEAD_DIM)).reshape(1, -1)
    return pl.pallas_call(
        functools.partial(_compress_body, transposed=transposed),
        out_shape=out_shape,
        grid=(n_sel // CMP_TILE,),
        in_specs=[pl.BlockSpec((CMP_TILE, ext.shape[1]), lambda tt: (tt, 0)),
                  _resident(pose.shape), _resident(w1e.shape), _resident(w2e.shape)],
        out_specs=out_spec,
        compiler_params=_cparams(("parallel",)),
        name="compress_v" if transposed else "compress_k",
    )(ext, pose, w1e, w2e)


def _cmp_topk_tiles(qt_ref, kc_ref, vct_ref, oct_ref, bias_ref, *, n_tiles, n_sel, n_top):
    i = pl.program_id(0)
    n_rows = 4 * CMP_TILE * n_tiles
    n_blk = CMP_TILE * n_tiles
    lane_t = i * Q_BLOCK + lax.broadcasted_iota(jnp.int32, (1, Q_BLOCK), 1)
    r = lax.broadcasted_iota(jnp.int32, (n_rows, Q_BLOCK), 0)
    n_of_row = CMP_TILE * (r // (4 * CMP_TILE)) + r % CMP_TILE
    j_of_row = (r % (4 * CMP_TILE)) // CMP_TILE
    valid = SEL_BLOCK * n_of_row + CMP_STRIDE * j_of_row + (CMP_BLOCK - 1) <= lane_t
    has_valid = lane_t >= CMP_BLOCK - 1
    blk = lax.broadcasted_iota(jnp.int32, (n_blk, Q_BLOCK), 0)
    blk_f = blk.astype(F32)
    cur = lane_t // SEL_BLOCK
    forced = (blk == cur) | (blk == 0)
    future = blk > cur

    scores = []
    for k in range(NSA_KV_HEADS):
        kc = kc_ref[0:n_rows, LANES * k:LANES * k + HEAD_DIM]
        vct = vct_ref[HEAD_DIM * k:HEAD_DIM * (k + 1), 0:n_rows]
        imp = jnp.zeros((n_blk, Q_BLOCK), F32)
        raw = [_dot(kc, qt_ref[HEAD_DIM * (NSA_GROUP * k + g):HEAD_DIM * (NSA_GROUP * k + g + 1), :])
               for g in range(NSA_GROUP)]
        for g in range(NSA_GROUP):
            h = NSA_GROUP * k + g
            s = jnp.where(valid, raw[g], NEG_INF)
            m = jnp.max(s, axis=0, keepdims=True)
            e = jnp.exp2(s - m)
            den = jnp.maximum(jnp.sum(e, axis=0, keepdims=True), 1e-30)
            inv = jnp.where(has_valid, 1.0 / den, 0.0)
            oct_ref[HEAD_DIM * h:HEAD_DIM * (h + 1), :] = _dot(vct, e.astype(BF16)) * inv
            parts = []
            for tt in range(n_tiles):
                base = 4 * CMP_TILE * tt
                acc = e[base:base + CMP_TILE, :]
                for j in range(1, 4):
                    acc = acc + e[base + j * CMP_TILE:base + (j + 1) * CMP_TILE, :]
                parts.append(acc)
            imp = imp + jnp.concatenate(parts, axis=0) * inv
        scores.append(jnp.where(forced, FORCE_SCORE, jnp.where(future, -1.0, imp)))

    def pick_one(_, carry):
        out = []
        for v in carry:
            top = jnp.max(v, axis=0, keepdims=True)
            first = jnp.min(jnp.where(v == top, blk_f, float(n_blk)), axis=0, keepdims=True)
            out.append(jnp.where(blk_f == first, -jnp.inf, v))
        return tuple(out)

    res = lax.fori_loop(0, n_top, pick_one, tuple(scores))
    for k in range(NSA_KV_HEADS):
        bias_ref[k, 0:n_blk, :] = jnp.where(res[k] == -jnp.inf, 0.0, NEG_INF)
        if n_blk < n_sel:
            bias_ref[k, n_blk:n_sel, :] = jnp.full((n_sel - n_blk, Q_BLOCK), NEG_INF, F32)


def _cmp_topk_body(qt_ref, kc_ref, vct_ref, oct_ref, bias_ref, *, n_sel, n_top):
    total = n_sel // CMP_TILE
    need = jnp.minimum((2 * pl.program_id(0) + 1) // CMP_TILE + 1, total)
    for n_tiles in range(1, total + 1):
        pl.when(need == n_tiles)(functools.partial(
            _cmp_topk_tiles, qt_ref, kc_ref, vct_ref, oct_ref, bias_ref,
            n_tiles=n_tiles, n_sel=n_sel, n_top=n_top))


def _cmp_topk(qt, kc, vct):
    s = qt.shape[1]
    n_sel = s // SEL_BLOCK
    n_top = min(SEL_TOPN, n_sel)
    qspec = pl.BlockSpec((NSA_Q_W, Q_BLOCK), lambda i: (0, i))
    return pl.pallas_call(
        functools.partial(_cmp_topk_body, n_sel=n_sel, n_top=n_top),
        out_shape=[jax.ShapeDtypeStruct((NSA_Q_W, s), F32),
                   jax.ShapeDtypeStruct((NSA_KV_HEADS, n_sel, s), F32)],
        grid=(s // Q_BLOCK,),
        in_specs=[qspec, _resident(kc.shape), _resident(vct.shape)],
        out_specs=[qspec, pl.BlockSpec((NSA_KV_HEADS, n_sel, Q_BLOCK), lambda i: (0, 0, i))],
        compiler_params=_cparams(("parallel",)),
        name="cmp_topk",
    )(qt, kc, vct)


def _band_body(*refs, mode, k_off, v_off):
    if mode == "sel":
        qt_ref, k_ref, vt_ref, bias_ref, o_ref, acc_s, ml_s, s_s = refs
    else:
        qt_ref, k_ref, vt_ref, o_ref, acc_s, ml_s, s_s = refs
        bias_ref = None
    i = pl.program_id(0)
    wide = NSA_GROUP * BAND_Q
    blocks = KEY_CHUNK // SEL_BLOCK
    lane = lax.broadcasted_iota(jnp.int32, (1, wide), 1)
    t = i * BAND_Q + lane % BAND_Q
    krow = lax.broadcasted_iota(jnp.int32, (KEY_CHUNK, wide), 0)
    c_hi = (i * BAND_Q + (BAND_Q - 1)) // KEY_CHUNK + 1

    acc_s[...] = jnp.zeros_like(acc_s)
    ml_s[0:NSA_KV_HEADS, :] = jnp.full((NSA_KV_HEADS, wide), NEG_INF, F32)
    ml_s[NSA_KV_HEADS:, :] = jnp.zeros((ml_s.shape[0] - NSA_KV_HEADS, wide), F32)

    def scores(c, k):
        base = pl.multiple_of(c * KEY_CHUNK, KEY_CHUNK)
        q4 = jnp.concatenate(
            [qt_ref[HEAD_DIM * (NSA_GROUP * k + g):HEAD_DIM * (NSA_GROUP * k + g + 1), :]
             for g in range(NSA_GROUP)], axis=1)
        keys = k_ref[pl.ds(base, KEY_CHUNK), k_off + LANES * k:k_off + LANES * k + HEAD_DIM]
        s_s[k] = _dot(keys, q4)

    def step(c, masked, prefetch):
        base = pl.multiple_of(c * KEY_CHUNK, KEY_CHUNK)
        for k in range(NSA_KV_HEADS):
            s = s_s[k]
            if mode == "sel":
                rows = []
                for rblk in range(blocks):
                    b = bias_ref[k, pl.ds(c * blocks + rblk, 1), :]
                    b = jnp.concatenate([b] * NSA_GROUP, axis=1)
                    rows.append(jnp.broadcast_to(b, (SEL_BLOCK, wide)))
                s = s + jnp.concatenate(rows, axis=0)
                if masked:
                    s = jnp.where(base + krow <= t, s, NEG_INF)
            else:
                dist = t - (base + krow)
                s = jnp.where((dist >= 0) & (dist < WIN), s, NEG_INF)
            m_old = ml_s[k:k + 1, :]
            m_new = jnp.maximum(m_old, jnp.max(s, axis=0, keepdims=True))
            alpha = jnp.exp2(m_old - m_new)
            e = jnp.exp2(s - m_new)
            ml_s[k:k + 1, :] = m_new
            lrow = NSA_KV_HEADS + k
            ml_s[lrow:lrow + 1, :] = alpha * ml_s[lrow:lrow + 1, :] + jnp.sum(e, axis=0, keepdims=True)
            vt = vt_ref[c, v_off + HEAD_DIM * k:v_off + HEAD_DIM * (k + 1), :]
            acc_s[k] = alpha * acc_s[k] + _dot(vt, e.astype(BF16))
            if prefetch:
                scores(c + 1, k)

    if mode == "sel":
        c_lo = 0
    else:
        c_lo = jnp.maximum(i * BAND_Q - (WIN - 1), 0) // KEY_CHUNK
    for k in range(NSA_KV_HEADS):
        scores(c_lo, k)
    lax.fori_loop(c_lo, c_hi - 1, lambda c, carry: (step(c, mode == "win", True), carry)[1], 0)
    step(c_hi - 1, True, False)

    for k in range(NSA_KV_HEADS):
        lrow = NSA_KV_HEADS + k
        out = acc_s[k] * (1.0 / jnp.maximum(ml_s[lrow:lrow + 1, :], 1e-30))
        for g in range(NSA_GROUP):
            h = NSA_GROUP * k + g
            o_ref[HEAD_DIM * h:HEAD_DIM * (h + 1), :] = out[:, BAND_Q * g:BAND_Q * (g + 1)]


def _band(qt, ksw, vswt, sel, *, mode):
    s = qt.shape[1]
    assert BAND_Q % KEY_CHUNK == 0 and s % BAND_Q == 0
    qspec = pl.BlockSpec((NSA_Q_W, BAND_Q), lambda i: (0, i))
    k_off = 0 if mode == "sel" else NSA_KV_HEADS * LANES
    v_off = 0 if mode == "sel" else NSA_KV_W
    in_specs = [qspec, _resident(ksw.shape), _resident(vswt.shape)]
    args = [qt, ksw, vswt]
    if mode == "sel":
        in_specs.append(pl.BlockSpec((NSA_KV_HEADS, sel.shape[1], BAND_Q), lambda i: (0, 0, i)))
        args.append(sel)
    wide = NSA_GROUP * BAND_Q
    return pl.pallas_call(
        functools.partial(_band_body, mode=mode, k_off=k_off, v_off=v_off),
        out_shape=jax.ShapeDtypeStruct((NSA_Q_W, s), F32),
        grid=(s // BAND_Q,),
        in_specs=in_specs,
        out_specs=qspec,
        scratch_shapes=[pltpu.VMEM((NSA_KV_HEADS, HEAD_DIM, wide), F32), pltpu.VMEM((8, wide), F32),
                        pltpu.VMEM((NSA_KV_HEADS, KEY_CHUNK, wide), F32)],
        compiler_params=_cparams(("parallel",)),
        name="nsa_" + mode,
    )(*args)


def _dil_body(q_ref, kp_ref, kc_ref, vp_ref, vc_ref, o_ref, lse_ref, *, span):
    i = pl.program_id(1)
    rq = lax.broadcasted_iota(jnp.int32, (Q_BLOCK, 2 * Q_BLOCK), 0)
    ck = lax.broadcasted_iota(jnp.int32, (Q_BLOCK, 2 * Q_BLOCK), 1)
    delta = Q_BLOCK + rq - ck
    key_idx = (i - 1) * Q_BLOCK + ck
    valid = (delta >= 0) & (delta <= span) & (key_idx >= 0)
    raw = []
    for j in range(DIL_HEADS_PER_GROUP):
        q = q_ref[:, LANES * j:LANES * j + HEAD_DIM]
        kk = jnp.concatenate([kp_ref[:, LANES * j:LANES * j + HEAD_DIM],
                              kc_ref[:, LANES * j:LANES * j + HEAD_DIM]], axis=0)
        raw.append(_dot_nt(q, kk))
    for j in range(DIL_HEADS_PER_GROUP):
        vv = jnp.concatenate([vp_ref[:, LANES * j:LANES * (j + 1)],
                              vc_ref[:, LANES * j:LANES * (j + 1)]], axis=0)
        s = jnp.where(valid, raw[j], NEG_INF)
        m = jnp.max(s, axis=1, keepdims=True)
        e = jnp.where(valid, jnp.exp(s - m), 0.0)
        den = jnp.maximum(jnp.sum(e, axis=1, keepdims=True), 1e-30)
        p = e * (1.0 / den)
        o_ref[:, LANES * j:LANES * (j + 1)] = _dot(p.astype(BF16), vv)
        lse_ref[:, LANES * j:LANES * (j + 1)] = jnp.broadcast_to(m + jnp.log(den), (Q_BLOCK, LANES))


def _dilated_group(c, gi):
    s = c.shape[0]
    window, dil = DIL_GROUPS[gi]
    span = window // dil
    assert span == Q_BLOCK
    rows = s // dil
    view = c.reshape(rows, dil * DIL_GROUP_W)
    spec = lambda tensor, prev: pl.BlockSpec(
        (Q_BLOCK, DIL_BLOCK_W),
        lambda r, i: (jnp.maximum(i - 1, 0) if prev else i, 3 * r + tensor))
    ospec = pl.BlockSpec((Q_BLOCK, DIL_BLOCK_W), lambda r, i: (i, r))
    o, lse = pl.pallas_call(
        functools.partial(_dil_body, span=span),
        out_shape=[jax.ShapeDtypeStruct((rows, dil * DIL_BLOCK_W), F32)] * 2,
        grid=(dil, rows // Q_BLOCK),
        in_specs=[spec(0, False), spec(1, True), spec(1, False), spec(2, True), spec(2, False)],
        out_specs=[ospec, ospec],
        compiler_params=_cparams(("parallel", "parallel")),
        name="dilated_%d" % gi,
    )(view, view, view, view, view)
    return o.reshape(s, DIL_BLOCK_W), lse.reshape(s, DIL_BLOCK_W)


def _merge_body(x_ref, g_ref, ya_ref, oct_ref, ost_ref, owt_ref, glt_ref,
                o0_ref, l0_ref, o1_ref, l1_ref, o2_ref, l2_ref,
                wm_ref, wua_ref, wub_ref, wuc_ref, wo_ref, out_ref):
    x = x_ref[...]
    h = _rms(x, g_ref[...]).astype(BF16)
    gates = jax.nn.sigmoid(_dot(h, wm_ref[...]))
    y_a = _dot(ya_ref[...], wua_ref[...])

    bg = jax.nn.sigmoid(glt_ref[...])
    pieces = []
    for hd in range(NSA_HEADS):
        rows = slice(HEAD_DIM * hd, HEAD_DIM * (hd + 1))
        pieces.append(bg[3 * hd:3 * hd + 1, :] * oct_ref[rows, :]
                      + bg[3 * hd + 1:3 * hd + 2, :] * ost_ref[rows, :]
                      + bg[3 * hd + 2:3 * hd + 3, :] * owt_ref[rows, :])
    yb_t = jnp.concatenate(pieces, axis=0)
    y_b = _dot(yb_t.T.astype(BF16), wub_ref[...])

    l0, l1, l2 = l0_ref[...], l1_ref[...], l2_ref[...]
    mx = jnp.maximum(jnp.maximum(l0, l1), l2)
    e0, e1, e2 = jnp.exp(l0 - mx), jnp.exp(l1 - mx), jnp.exp(l2 - mx)
    inv = 1.0 / (e0 + e1 + e2)
    yc = (e0 * inv) * o0_ref[...] + (e1 * inv) * o1_ref[...] + (e2 * inv) * o2_ref[...]
    y_c = _dot(yc.astype(BF16), wuc_ref[...])

    d = x.shape[1]
    merged = gates[:, 0:d] * y_a + gates[:, d:2 * d] * y_b + gates[:, 2 * d:3 * d] * y_c
    out_ref[...] = x + _dot(merged.astype(BF16), wo_ref[...])


def _merge(x, g, ya, oct_, ost, owt, glt, dil, w_in, w_up_a, w_up_b, w_up_c, w_out):
    s, d = x.shape
    tm = min(MERGE_ROWS, s)
    w_m = w_in[:, IN_OFFSETS[12]:].astype(BF16)
    wuc = w_up_c.reshape(DIL_HEADS_PER_GROUP, HEAD_DIM, d)
    wuc = jnp.pad(wuc, ((0, 0), (0, LANES - HEAD_DIM), (0, 0))).reshape(DIL_BLOCK_W, d).astype(BF16)
    weights = [w_m, w_up_a.astype(BF16), w_up_b.astype(BF16), wuc, w_out.astype(BF16)]

    def rows(n):
        return pl.BlockSpec((tm, n), lambda i: (i, 0))

    def cols(n):
        return pl.BlockSpec((n, tm), lambda i: (0, i))

    dil_args, dil_specs = [], []
    for o, lse in dil:
        dil_args += [o, lse]
        dil_specs += [rows(DIL_BLOCK_W), rows(DIL_BLOCK_W)]
    return pl.pallas_call(
        _merge_body,
        out_shape=jax.ShapeDtypeStruct((s, d), F32),
        grid=(s // tm,),
        in_specs=[rows(d), _resident((1, d)), rows(LRU_WIDTH), cols(NSA_Q_W), cols(NSA_Q_W),
                  cols(NSA_Q_W), cols(GATE_ROWS)] + dil_specs + [_resident(w.shape) for w in weights],
        out_specs=rows(d),
        compiler_params=_cparams(("parallel",)),
        name="merge",
    )(x, g.reshape(1, d), ya, oct_, ost, owt, glt, *dil_args, *weights)


def _mixer(x, mix_norm, w_in, conv_w, conv_b, lru_wa, lru_ba, lru_wi, lru_bi, lru_lambda,
           cmp_pos_k, cmp_pos_v, cmp_k_w1, cmp_k_w2, cmp_v_w1, cmp_v_w2,
           w_up_a, w_up_b, w_up_c, w_out):
    a2, qt, kvc, ksw, vswt, glt, *c = _proj(x, mix_norm, w_in)
    ya = _lru(a2, conv_w, conv_b, lru_wa, lru_ba, lru_wi, lru_bi, lru_lambda)
    kc = _compress(kvc[:, :NSA_KV_W], cmp_pos_k, cmp_k_w1, cmp_k_w2, transposed=False)
    vct = _compress(kvc[:, NSA_KV_W:], cmp_pos_v, cmp_v_w1, cmp_v_w2, transposed=True)
    oct_, sel = _cmp_topk(qt, kc, vct)
    ost = _band(qt, ksw, vswt, sel, mode="sel")
    owt = _band(qt, ksw, vswt, None, mode="win")
    dil = [_dilated_group(c[gi], gi) for gi in range(len(DIL_GROUPS))]
    return _merge(x, mix_norm, ya, oct_, ost, owt, glt, dil, w_in, w_up_a, w_up_b, w_up_c, w_out)


def kernel(x, ffn1_norm, ffn1_w1, ffn1_w3, ffn1_w2, mix_norm, w_in, conv_w, conv_b, lru_wa, lru_ba, lru_wi, lru_bi, lru_lambda, cmp_pos_k, cmp_pos_v, cmp_k_w1, cmp_k_w2, cmp_v_w1, cmp_v_w2, w_up_a, w_up_b, w_up_c, w_out, ffn2_norm, ffn2_w1, ffn2_w3, ffn2_w2, final_norm):
    batch, s, d = x.shape
    assert batch == 1 and d == D_MODEL
    depth = w_in.shape[0]
    y = x.reshape(s, d)
    for l in range(depth):
        y = _ffn(y, ffn1_norm[l], ffn1_w1[l], ffn1_w3[l], ffn1_w2[l], final_norm, final=False)
        y = _mixer(y, mix_norm[l], w_in[l], conv_w[l], conv_b[l], lru_wa[l], lru_ba[l], lru_wi[l],
                   lru_bi[l], lru_lambda[l], cmp_pos_k[l], cmp_pos_v[l], cmp_k_w1[l], cmp_k_w2[l],
                   cmp_v_w1[l], cmp_v_w2[l], w_up_a[l], w_up_b[l], w_up_c[l], w_out[l])
        y = _ffn(y, ffn2_norm[l], ffn2_w1[l], ffn2_w3[l], ffn2_w2[l], final_norm,
                 final=(l == depth - 1))
    return y.reshape(batch, s, d)
```

```python
import functools

import jax
import jax.numpy as jnp
import numpy as np
from jax import lax
from jax.experimental import pallas as pl
from jax.experimental.pallas import tpu as pltpu

F32 = jnp.float32
BF16 = jnp.bfloat16

D_MODEL = 1024
D_FF = 2816
HEAD_DIM = 64
RMS_EPS = 1e-6
NEG_INF = -1e30
SCALE = HEAD_DIM ** -0.5
QK_SCALE_LOG2 = float(SCALE * np.log2(np.e))

LRU_WIDTH = 768
CONV_WIDTH = 4
LRU_C = 8.0

NSA_HEADS = 12
NSA_KV_HEADS = 3
NSA_GROUP = 4
NSA_Q_W = 768
NSA_KV_W = 192
CMP_BLOCK = 32
CMP_STRIDE = 16
SEL_BLOCK = 64
SEL_TOPN = 16
WIN = 512
FORCE_SCORE = 1e9

DIL_GROUPS = ((128, 1), (512, 4), (2048, 16))
DIL_HEADS_PER_GROUP = 4
DIL_W = 768

IN_SPLITS = (768, 768, 768, 192, 192, 192, 192, 192, 192, 36, 768, 768, 768, 1024, 1024, 1024)
IN_OFFSETS = tuple(int(o) for o in np.cumsum(IN_SPLITS)[:-1])

LANES = 128
V7X_VMEM_LIMIT_BYTES = 56 * 1024 * 1024

FF_CHUNK = 256
FFN_ROWS = 1024
PROJ_ROWS = 256
LRU_ROWS = 256
MERGE_ROWS = 256
Q_BLOCK = 128
KEY_CHUNK = 256
BAND_Q = 256
SUM_ROWS = 16
GATE_ROWS = 40
CMP_TILE = 64
DIL_BLOCK_W = DIL_HEADS_PER_GROUP * LANES
DIL_GROUP_W = 3 * DIL_BLOCK_W


def _cparams(sem):
    return pltpu.CompilerParams(dimension_semantics=sem, vmem_limit_bytes=V7X_VMEM_LIMIT_BYTES)


def _resident(shape):
    nd = len(shape)
    return pl.BlockSpec(shape, lambda *_: (0,) * nd, pipeline_mode=pl.Buffered(1))


def _rms(x, g):
    return x * lax.rsqrt(jnp.mean(x * x, axis=-1, keepdims=True) + RMS_EPS) * g


def _gelu_tanh(x):
    c = np.float32(np.sqrt(2.0 / np.pi))
    return x * (0.5 * (1.0 + jnp.tanh(c * (x + 0.044715 * (x * x * x)))))


def _dot(a, b):
    return jnp.dot(a, b, preferred_element_type=F32)


def _dot_nt(a, b):
    return lax.dot_general(a, b, (((1,), (1,)), ((), ())), preferred_element_type=F32)


def _ffn_body(x_ref, g_ref, w1_ref, w3_ref, w2_ref, fg_ref, o_ref, h_ref, acc_ref, *, final):
    x = x_ref[...]
    h_ref[...] = _rms(x, g_ref[...]).astype(BF16)
    acc_ref[...] = jnp.zeros_like(acc_ref)

    def chunk(c, carry):
        h = h_ref[...]
        a = _dot(h, w1_ref[c])
        b = _dot(h, w3_ref[c])
        gated = (a * jax.nn.sigmoid(a) * b).astype(BF16)
        acc_ref[...] += _dot(gated, w2_ref[c])
        return carry

    lax.fori_loop(0, w1_ref.shape[0], chunk, 0)
    y = x + 0.5 * acc_ref[...]
    if final:
        y = _rms(y, fg_ref[...])
    o_ref[...] = y


def _ffn(x, g, w1, w3, w2, fg, *, final):
    s, d = x.shape
    nc = D_FF // FF_CHUNK
    w1c = w1.astype(BF16).reshape(d, nc, FF_CHUNK).transpose(1, 0, 2)
    w3c = w3.astype(BF16).reshape(d, nc, FF_CHUNK).transpose(1, 0, 2)
    w2c = w2.astype(BF16).reshape(nc, FF_CHUNK, d)
    tm = min(FFN_ROWS, s)
    row = pl.BlockSpec((tm, d), lambda i: (i, 0))
    return pl.pallas_call(
        functools.partial(_ffn_body, final=final),
        out_shape=jax.ShapeDtypeStruct((s, d), F32),
        grid=(s // tm,),
        in_specs=[row, _resident((1, d)), _resident(w1c.shape), _resident(w3c.shape),
                  _resident(w2c.shape), _resident((1, d))],
        out_specs=row,
        scratch_shapes=[pltpu.VMEM((tm, d), BF16), pltpu.VMEM((tm, d), F32)],
        compiler_params=_cparams(("parallel",)),
        name="ffn_final" if final else "ffn",
    )(x, g.reshape(1, d), w1c, w3c, w2c, fg.reshape(1, d))


def _proj_body(x_ref, g_ref, w_a_ref, w_q_ref, w_kvc_ref, w_ksw_ref, w_vsw_ref, w_gl_ref,
               w_c0_ref, w_c1_ref, w_c2_ref,
               a2_ref, qt_ref, kvc_ref, ksw_ref, vswt_ref, glt_ref, c0_ref, c1_ref, c2_ref, slab_s):
    tm = x_ref.shape[0]
    h = _rms(x_ref[...], g_ref[...]).astype(BF16)
    a2_ref[...] = _dot(h, w_a_ref[...])
    qt_ref[...] = (_dot(h, w_q_ref[...]) * QK_SCALE_LOG2).T.astype(BF16)
    kvc_ref[...] = _dot(h, w_kvc_ref[...])
    ksw_ref[...] = _dot(h, w_ksw_ref[...]).astype(BF16)
    vswt_ref[0] = _dot(h, w_vsw_ref[...]).T.astype(BF16)
    glt_ref[...] = _dot(h, w_gl_ref[...]).T[0:GATE_ROWS, :]
    for w_ref, c_ref, (_, dil) in zip((w_c0_ref, w_c1_ref, w_c2_ref), (c0_ref, c1_ref, c2_ref), DIL_GROUPS):
        for part, scale in ((slice(0, DIL_BLOCK_W), SCALE), (slice(DIL_BLOCK_W, DIL_GROUP_W), 1.0)):
            res = _dot(h, w_ref[:, part])
            if scale != 1.0:
                res = res * scale
            if dil == 1:
                c_ref[:, part] = res.astype(BF16)
                continue
            for sl in range(part.start // LANES, part.stop // LANES):
                slab_s[sl] = res[:, LANES * sl - part.start:LANES * (sl + 1) - part.start]
                for r in range(dil):
                    col = r * DIL_GROUP_W + LANES * sl
                    c_ref[:, col:col + LANES] = slab_s[sl, pl.ds(r, tm // dil, stride=dil), :].astype(BF16)


def _pad_heads(w, n_heads):
    k = w.shape[0]
    w = w.reshape(k, n_heads, HEAD_DIM)
    w = jnp.pad(w, ((0, 0), (0, 0), (0, LANES - HEAD_DIM)))
    return w.reshape(k, n_heads * LANES)


def _proj(x, g, w_in):
    s, d = x.shape
    (a_x, a_gate, b_q, b_kc, b_vc, b_ks, b_vs, b_kw, b_vw, b_gate,
     c_q, c_k, c_v, _, _, _) = jnp.split(w_in, list(IN_OFFSETS), axis=1)
    w_a = jnp.concatenate([a_x, a_gate], axis=1)
    w_kvc = jnp.concatenate([b_kc, b_vc], axis=1)
    w_ksw = jnp.concatenate([_pad_heads(b_ks, 3), _pad_heads(b_kw, 3)], axis=1)
    w_vsw = jnp.concatenate([b_vs, b_vw], axis=1)
    w_gl = jnp.pad(b_gate, ((0, 0), (0, LANES - b_gate.shape[1])))
    gw = DIL_HEADS_PER_GROUP * HEAD_DIM
    w_c = [jnp.concatenate([_pad_heads(t[:, gi * gw:(gi + 1) * gw], DIL_HEADS_PER_GROUP)
                            for t in (c_q, c_k, c_v)], axis=1) for gi in range(len(DIL_GROUPS))]
    weights = [w.astype(BF16) for w in [w_a, b_q, w_kvc, w_ksw, w_vsw, w_gl] + w_c]
    tm = PROJ_ROWS
    assert tm == KEY_CHUNK and s % tm == 0

    def rows(n):
        return pl.BlockSpec((tm, n), lambda i: (i, 0))

    def cols(n):
        return pl.BlockSpec((n, tm), lambda i: (0, i))

    out_shape = [
        jax.ShapeDtypeStruct((s, 2 * LRU_WIDTH), F32),
        jax.ShapeDtypeStruct((NSA_Q_W, s), BF16),
        jax.ShapeDtypeStruct((s, 2 * NSA_KV_W), F32),
        jax.ShapeDtypeStruct((s, 6 * LANES), BF16),
        jax.ShapeDtypeStruct((s // tm, 2 * NSA_KV_W, tm), BF16),
        jax.ShapeDtypeStruct((GATE_ROWS, s), F32),
    ]
    out_specs = [rows(2 * LRU_WIDTH), cols(NSA_Q_W), rows(2 * NSA_KV_W), rows(6 * LANES),
                 pl.BlockSpec((1, 2 * NSA_KV_W, tm), lambda i: (i, 0, 0)), cols(GATE_ROWS)]
    for _, dil in DIL_GROUPS:
        out_shape.append(jax.ShapeDtypeStruct((s // dil, dil * DIL_GROUP_W), BF16))
        out_specs.append(pl.BlockSpec((tm // dil, dil * DIL_GROUP_W), lambda i: (i, 0)))
    return pl.pallas_call(
        _proj_body,
        out_shape=out_shape,
        grid=(s // tm,),
        in_specs=[rows(d), _resident((1, d))] + [_resident(w.shape) for w in weights],
        out_specs=out_specs,
        scratch_shapes=[pltpu.VMEM((DIL_GROUP_W // LANES, tm, LANES), F32)],
        compiler_params=_cparams(("parallel",)),
        name="proj",
    )(x, g.reshape(1, d), *weights)


def _lru_body(a2_ref, cw_ref, cb_ref, wa_ref, ba_ref, wi_ref, bi_ref, lam_ref, y_ref,
              xbuf, a_s, u_s, h_s, hc):
    t_rows = y_ref.shape[0]
    w = LRU_WIDTH

    @pl.when(pl.program_id(0) == 0)
    def _():
        xbuf[0:8, :] = jnp.zeros((8, w), F32)
        hc[...] = jnp.zeros_like(hc)

    x = a2_ref[:, 0:w]
    xbuf[8:8 + t_rows, :] = x
    cw = cw_ref[...]
    xc = cb_ref[...] + cw[0:1] * xbuf[5:5 + t_rows, :]
    xc = xc + cw[1:2] * xbuf[6:6 + t_rows, :]
    xc = xc + cw[2:3] * xbuf[7:7 + t_rows, :]
    xc = xc + cw[3:4] * x
    xbuf[0:8, :] = x[t_rows - 8:t_rows, :]

    xcb = xc.astype(BF16)
    ra, ri = [], []
    for p in range(wa_ref.shape[0]):
        blk = xcb[:, 256 * p:256 * (p + 1)]
        ra.append(_dot(blk, wa_ref[p]))
        ri.append(_dot(blk, wi_ref[p]))
    r = jax.nn.sigmoid(jnp.concatenate(ra, axis=1) + ba_ref[...])
    ig = jax.nn.sigmoid(jnp.concatenate(ri, axis=1) + bi_ref[...])
    z = -lam_ref[...]
    softplus = jnp.maximum(z, 0.0) + jnp.log1p(jnp.exp(-jnp.abs(z)))
    log_a = (-LRU_C * r) * softplus
    a_s[...] = jnp.exp(log_a)
    u_s[...] = jnp.sqrt(1.0 - jnp.exp(2.0 * log_a)) * (ig * xc)

    row = lax.broadcasted_iota(jnp.int32, (8, w), 0)

    def group(gidx, hprev):
        base = pl.multiple_of(gidx * 8, 8)
        a = a_s[pl.ds(base, 8), :]
        b = u_s[pl.ds(base, 8), :]
        for dshift in (1, 2, 4):
            ok = row >= dshift
            a_sh = pltpu.roll(a, dshift, 0)
            b_sh = pltpu.roll(b, dshift, 0)
            b = jnp.where(ok, a * b_sh + b, b)
            a = jnp.where(ok, a * a_sh, a)
        h8 = a * hprev + b
        h_s[pl.ds(base, 8), :] = h8
        return jnp.broadcast_to(h8[7:8, :], (8, w))

    hc[...] = lax.fori_loop(0, t_rows // 8, group, hc[...])
    y_ref[...] = (h_s[...] * _gelu_tanh(a2_ref[:, w:2 * w])).astype(BF16)


def _pair_block_diag(wb):
    z = jnp.zeros((128, 128), wb.dtype)
    return jnp.stack([jnp.block([[wb[2 * p], z], [z, wb[2 * p + 1]]]) for p in range(3)])


def _lru(a2, conv_w, conv_b, wa, ba, wi, bi, lam):
    s = a2.shape[0]
    w = LRU_WIDTH
    t = min(LRU_ROWS, s)
    wa2 = _pair_block_diag(wa.astype(BF16))
    wi2 = _pair_block_diag(wi.astype(BF16))
    vec = _resident((1, w))
    return pl.pallas_call(
        _lru_body,
        out_shape=jax.ShapeDtypeStruct((s, w), BF16),
        grid=(s // t,),
        in_specs=[pl.BlockSpec((t, 2 * w), lambda i: (i, 0)), _resident((CONV_WIDTH, w)), vec,
                  _resident(wa2.shape), vec, _resident(wi2.shape), vec, vec],
        out_specs=pl.BlockSpec((t, w), lambda i: (i, 0)),
        scratch_shapes=[pltpu.VMEM((t + 8, w), F32), pltpu.VMEM((t, w), F32), pltpu.VMEM((t, w), F32),
                        pltpu.VMEM((t, w), F32), pltpu.VMEM((8, w), F32)],
        compiler_params=_cparams(("arbitrary",)),
        name="lru",
    )(a2, conv_w, conv_b.reshape(1, w), wa2, ba.reshape(1, w), wi2, bi.reshape(1, w), lam.reshape(1, w))


CMP_HALF = CMP_STRIDE * NSA_KV_W


def _compress_body(x_ref, pos_ref, w1_ref, w2_ref, o_ref, *, transposed):
    hids = []
    for j in range(4):
        xa = (x_ref[:, CMP_HALF * j:CMP_HALF * (j + 1)] + pos_ref[:, :CMP_HALF]).astype(BF16)
        xb = (x_ref[:, CMP_HALF * (j + 1):CMP_HALF * (j + 2)] + pos_ref[:, CMP_HALF:]).astype(BF16)
        pre = _dot(xa, w1_ref[:CMP_HALF, :]) + _dot(xb, w1_ref[CMP_HALF:, :])
        hids.append(_gelu_tanh(pre).astype(BF16))
    hid = jnp.concatenate(hids, axis=0)
    if transposed:
        o_ref[...] = _dot_nt(w2_ref[...], hid).astype(BF16)
    else:
        o_ref[...] = _dot(hid, w2_ref[...]).astype(BF16)


def _compress(kv, pos, w1, w2, *, transposed):
    s = kv.shape[0]
    n_sel = s // SEL_BLOCK
    assert n_sel % CMP_TILE == 0
    rows = kv.reshape(n_sel, SEL_BLOCK * NSA_KV_W)
    nxt = jnp.concatenate([rows[1:, :CMP_HALF], jnp.zeros((1, CMP_HALF), kv.dtype)], axis=0)
    ext = jnp.concatenate([rows, nxt], axis=1)
    eye = jnp.eye(NSA_KV_HEADS, dtype=F32)
    w1e = jnp.einsum("ldn,hg->lhdgn", w1.reshape(CMP_BLOCK, HEAD_DIM, -1), eye)
    w1e = w1e.reshape(CMP_BLOCK * NSA_KV_W, -1).astype(BF16)
    hid = w1.shape[1]
    if transposed:
        w2e = jnp.einsum("nd,hg->gdhn", w2, eye).reshape(NSA_KV_W, NSA_KV_HEADS * hid).astype(BF16)
        out_shape = jax.ShapeDtypeStruct((NSA_KV_W, 4 * n_sel), BF16)
        out_spec = pl.BlockSpec((NSA_KV_W, 4 * CMP_TILE), lambda tt: (0, tt))
    else:
        w2p = jnp.pad(w2, ((0, 0), (0, LANES - HEAD_DIM)))
        w2e = jnp.einsum("nd,hg->hngd", w2p, eye).reshape(NSA_KV_HEADS * hid, NSA_KV_HEADS * LANES).astype(BF16)
        out_shape = jax.ShapeDtypeStruct((4 * n_sel, NSA_KV_HEADS * LANES), BF16)
        out_spec = pl.BlockSpec((4 * CMP_TILE, NSA_KV_HEADS * LANES), lambda tt: (tt, 0))
    pose = jnp.broadcast_to(pos[:, None, :], (CMP_BLOCK, NSA_KV_HEADS, HEAD_DIM)).reshape(1, -1)
    return pl.pallas_call(
        functools.partial(_compress_body, transposed=transposed),
        out_shape=out_shape,
        grid=(n_sel // CMP_TILE,),
        in_specs=[pl.BlockSpec((CMP_TILE, ext.shape[1]), lambda tt: (tt, 0)),
                  _resident(pose.shape), _resident(w1e.shape), _resident(w2e.shape)],
        out_specs=out_spec,
        compiler_params=_cparams(("parallel",)),
        name="compress_v" if transposed else "compress_k",
    )(ext, pose, w1e, w2e)


def _cmp_topk_tiles(qt_ref, kc_ref, vct_ref, oct_ref, bias_ref, *, n_tiles, n_sel, n_top):
    i = pl.program_id(0)
    n_rows = 4 * CMP_TILE * n_tiles
    n_blk = CMP_TILE * n_tiles
    lane_t = i * Q_BLOCK + lax.broadcasted_iota(jnp.int32, (1, Q_BLOCK), 1)
    r = lax.broadcasted_iota(jnp.int32, (n_rows, Q_BLOCK), 0)
    n_of_row = CMP_TILE * (r // (4 * CMP_TILE)) + r % CMP_TILE
    j_of_row = (r % (4 * CMP_TILE)) // CMP_TILE
    valid = SEL_BLOCK * n_of_row + CMP_STRIDE * j_of_row + (CMP_BLOCK - 1) <= lane_t
    has_valid = lane_t >= CMP_BLOCK - 1
    blk = lax.broadcasted_iota(jnp.int32, (n_blk, Q_BLOCK), 0)
    blk_f = blk.astype(F32)
    cur = lane_t // SEL_BLOCK
    forced = (blk == cur) | (blk == 0)
    future = blk > cur

    scores = []
    for k in range(NSA_KV_HEADS):
        kc = kc_ref[0:n_rows, LANES * k:LANES * k + HEAD_DIM]
        vct = jnp.concatenate([vct_ref[HEAD_DIM * k:HEAD_DIM * (k + 1), 0:n_rows],
                               jnp.ones((SUM_ROWS, n_rows), BF16)], axis=0)
        imp = jnp.zeros((n_blk, Q_BLOCK), F32)
        raw = [_dot(kc, qt_ref[HEAD_DIM * (NSA_GROUP * k + g):HEAD_DIM * (NSA_GROUP * k + g + 1), :])
               for g in range(NSA_GROUP)]
        for g in range(NSA_GROUP):
            h = NSA_GROUP * k + g
            s = jnp.where(valid, raw[g], NEG_INF)
            m = jnp.max(s, axis=0, keepdims=True)
            e = jnp.exp2(s - m)
            pv = _dot(vct, e.astype(BF16))
            den = jnp.maximum(pv[HEAD_DIM:HEAD_DIM + 1, :], 1e-30)
            inv = jnp.where(has_valid, 1.0 / den, 0.0)
            oct_ref[HEAD_DIM * h:HEAD_DIM * (h + 1), :] = pv[0:HEAD_DIM, :] * inv
            parts = []
            for tt in range(n_tiles):
                base = 4 * CMP_TILE * tt
                acc = e[base:base + CMP_TILE, :]
                for j in range(1, 4):
                    acc = acc + e[base + j * CMP_TILE:base + (j + 1) * CMP_TILE, :]
                parts.append(acc)
            imp = imp + jnp.concatenate(parts, axis=0) * inv
        scores.append(jnp.where(forced, FORCE_SCORE, jnp.where(future, -1.0, imp)))

    def pick_one(_, carry):
        out = []
        for v in carry:
            top = jnp.max(v, axis=0, keepdims=True)
            first = jnp.min(jnp.where(v == top, blk_f, float(n_blk)), axis=0, keepdims=True)
            out.append(jnp.where(blk_f == first, -jnp.inf, v))
        return tuple(out)

    res = lax.fori_loop(0, n_top, pick_one, tuple(scores))
    for k in range(NSA_KV_HEADS):
        bias_ref[k, 0:n_blk, :] = jnp.where(res[k] == -jnp.inf, 0.0, NEG_INF)
        if n_blk < n_sel:
            bias_ref[k, n_blk:n_sel, :] = jnp.full((n_sel - n_blk, Q_BLOCK), NEG_INF, F32)


def _cmp_topk_body(qt_ref, kc_ref, vct_ref, oct_ref, bias_ref, *, n_sel, n_top):
    total = n_sel // CMP_TILE
    need = jnp.minimum((2 * pl.program_id(0) + 1) // CMP_TILE + 1, total)
    for n_tiles in range(1, total + 1):
        pl.when(need == n_tiles)(functools.partial(
            _cmp_topk_tiles, qt_ref, kc_ref, vct_ref, oct_ref, bias_ref,
            n_tiles=n_tiles, n_sel=n_sel, n_top=n_top))


def _cmp_topk(qt, kc, vct):
    s = qt.shape[1]
    n_sel = s // SEL_BLOCK
    n_top = min(SEL_TOPN, n_sel)
    qspec = pl.BlockSpec((NSA_Q_W, Q_BLOCK), lambda i: (0, i))
    return pl.pallas_call(
        functools.partial(_cmp_topk_body, n_sel=n_sel, n_top=n_top),
        out_shape=[jax.ShapeDtypeStruct((NSA_Q_W, s), F32),
                   jax.ShapeDtypeStruct((NSA_KV_HEADS, n_sel, s), F32)],
        grid=(s // Q_BLOCK,),
        in_specs=[qspec, _resident(kc.shape), _resident(vct.shape)],
        out_specs=[qspec, pl.BlockSpec((NSA_KV_HEADS, n_sel, Q_BLOCK), lambda i: (0, 0, i))],
        compiler_params=_cparams(("parallel",)),
        name="cmp_topk",
    )(qt, kc, vct)


def _band_body(*refs, mode, k_off, v_off):
    if mode == "sel":
        qt_ref, k_ref, vt_ref, bias_ref, o_ref, acc_s, ml_s, s_s = refs
    else:
        qt_ref, k_ref, vt_ref, o_ref, acc_s, ml_s, s_s = refs
        bias_ref = None
    i = pl.program_id(0)
    wide = NSA_GROUP * BAND_Q
    blocks = KEY_CHUNK // SEL_BLOCK
    lane = lax.broadcasted_iota(jnp.int32, (1, wide), 1)
    t = i * BAND_Q + lane % BAND_Q
    krow = lax.broadcasted_iota(jnp.int32, (KEY_CHUNK, wide), 0)
    c_hi = (i * BAND_Q + (BAND_Q - 1)) // KEY_CHUNK + 1

    acc_s[...] = jnp.zeros_like(acc_s)
    ml_s[...] = jnp.full(ml_s.shape, NEG_INF, F32)
    ones = jnp.ones((SUM_ROWS, KEY_CHUNK), BF16)

    def scores(c, k):
        base = pl.multiple_of(c * KEY_CHUNK, KEY_CHUNK)
        q4 = jnp.concatenate(
            [qt_ref[HEAD_DIM * (NSA_GROUP * k + g):HEAD_DIM * (NSA_GROUP * k + g + 1), :]
             for g in range(NSA_GROUP)], axis=1)
        keys = k_ref[pl.ds(base, KEY_CHUNK), k_off + LANES * k:k_off + LANES * k + HEAD_DIM]
        s_s[k] = _dot(keys, q4)

    def softmax(c, k, masked):
        base = pl.multiple_of(c * KEY_CHUNK, KEY_CHUNK)
        s = s_s[k]
        if mode == "sel":
            rows = []
            for rblk in range(blocks):
                b = bias_ref[k, pl.ds(c * blocks + rblk, 1), :]
                b = jnp.concatenate([b] * NSA_GROUP, axis=1)
                rows.append(jnp.broadcast_to(b, (SEL_BLOCK, wide)))
            s = s + jnp.concatenate(rows, axis=0)
            if masked:
                s = jnp.where(base + krow <= t, s, NEG_INF)
        else:
            dist = t - (base + krow)
            s = jnp.where((dist >= 0) & (dist < WIN), s, NEG_INF)
        m_old = ml_s[k:k + 1, :]
        m_new = jnp.maximum(m_old, jnp.max(s, axis=0, keepdims=True))
        ml_s[k:k + 1, :] = m_new
        return jnp.exp2(m_old - m_new), jnp.exp2(s - m_new).astype(BF16)

    def accumulate(c, k, alpha, e):
        vt = jnp.concatenate([vt_ref[c, v_off + HEAD_DIM * k:v_off + HEAD_DIM * (k + 1), :], ones], axis=0)
        acc_s[k] = alpha * acc_s[k] + _dot(vt, e)

    def step(c, masked, prefetch):
        for k in range(NSA_KV_HEADS):
            alpha, e = softmax(c, k, masked)
            accumulate(c, k, alpha, e)
            if prefetch:
                scores(c + 1, k)

    if mode == "sel":
        c_lo = 0
    else:
        c_lo = jnp.maximum(i * BAND_Q - (WIN - 1), 0) // KEY_CHUNK
    for k in range(NSA_KV_HEADS):
        scores(c_lo, k)
    lax.fori_loop(c_lo, c_hi - 1, lambda c, carry: (step(c, mode == "win", True), carry)[1], 0)
    step(c_hi - 1, True, False)

    for k in range(NSA_KV_HEADS):
        den = jnp.maximum(acc_s[k, HEAD_DIM:HEAD_DIM + 1, :], 1e-30)
        out = acc_s[k, 0:HEAD_DIM, :] * (1.0 / den)
        for g in range(NSA_GROUP):
            h = NSA_GROUP * k + g
            o_ref[HEAD_DIM * h:HEAD_DIM * (h + 1), :] = out[:, BAND_Q * g:BAND_Q * (g + 1)]


def _band(qt, ksw, vswt, sel, *, mode):
    s = qt.shape[1]
    assert (BAND_Q % KEY_CHUNK == 0 or KEY_CHUNK % BAND_Q == 0) and s % BAND_Q == 0
    qspec = pl.BlockSpec((NSA_Q_W, BAND_Q), lambda i: (0, i))
    k_off = 0 if mode == "sel" else NSA_KV_HEADS * LANES
    v_off = 0 if mode == "sel" else NSA_KV_W
    in_specs = [qspec, _resident(ksw.shape), _resident(vswt.shape)]
    args = [qt, ksw, vswt]
    if mode == "sel":
        in_specs.append(pl.BlockSpec((NSA_KV_HEADS, sel.shape[1], BAND_Q), lambda i: (0, 0, i)))
        args.append(sel)
    wide = NSA_GROUP * BAND_Q
    return pl.pallas_call(
        functools.partial(_band_body, mode=mode, k_off=k_off, v_off=v_off),
        out_shape=jax.ShapeDtypeStruct((NSA_Q_W, s), F32),
        grid=(s // BAND_Q,),
        in_specs=in_specs,
        out_specs=qspec,
        scratch_shapes=[pltpu.VMEM((NSA_KV_HEADS, HEAD_DIM + SUM_ROWS, wide), F32), pltpu.VMEM((8, wide), F32),
                        pltpu.VMEM((NSA_KV_HEADS, KEY_CHUNK, wide), F32)],
        compiler_params=_cparams(("parallel",)),
        name="nsa_" + mode,
    )(*args)


def _dil_body(q_ref, kp_ref, kc_ref, vp_ref, vc_ref, o_ref, lse_ref, *, span):
    i = pl.program_id(1)
    rq = lax.broadcasted_iota(jnp.int32, (Q_BLOCK, 2 * Q_BLOCK), 0)
    ck = lax.broadcasted_iota(jnp.int32, (Q_BLOCK, 2 * Q_BLOCK), 1)
    delta = Q_BLOCK + rq - ck
    key_idx = (i - 1) * Q_BLOCK + ck
    valid = (delta >= 0) & (delta <= span) & (key_idx >= 0)
    raw = []
    for j in range(DIL_HEADS_PER_GROUP):
        q = q_ref[:, LANES * j:LANES * j + HEAD_DIM]
        kk = jnp.concatenate([kp_ref[:, LANES * j:LANES * j + HEAD_DIM],
                              kc_ref[:, LANES * j:LANES * j + HEAD_DIM]], axis=0)
        raw.append(_dot_nt(q, kk))
    for j in range(DIL_HEADS_PER_GROUP):
        vv = jnp.concatenate([vp_ref[:, LANES * j:LANES * (j + 1)],
                              vc_ref[:, LANES * j:LANES * (j + 1)]], axis=0)
        s = jnp.where(valid, raw[j], NEG_INF)
        m = jnp.max(s, axis=1, keepdims=True)
        e = jnp.where(valid, jnp.exp(s - m), 0.0)
        den = jnp.maximum(jnp.sum(e, axis=1, keepdims=True), 1e-30)
        p = e * (1.0 / den)
        o_ref[:, LANES * j:LANES * (j + 1)] = _dot(p.astype(BF16), vv)
        lse_ref[:, LANES * j:LANES * (j + 1)] = jnp.broadcast_to(m + jnp.log(den), (Q_BLOCK, LANES))


def _dilated_group(c, gi):
    window, dil = DIL_GROUPS[gi]
    span = window // dil
    assert span == Q_BLOCK
    view = c
    rows = view.shape[0]
    s = rows * dil
    spec = lambda tensor, prev: pl.BlockSpec(
        (Q_BLOCK, DIL_BLOCK_W),
        lambda r, i: (jnp.maximum(i - 1, 0) if prev else i, 3 * r + tensor))
    ospec = pl.BlockSpec((Q_BLOCK, DIL_BLOCK_W), lambda r, i: (i, r))
    o, lse = pl.pallas_call(
        functools.partial(_dil_body, span=span),
        out_shape=[jax.ShapeDtypeStruct((rows, dil * DIL_BLOCK_W), F32)] * 2,
        grid=(dil, rows // Q_BLOCK),
        in_specs=[spec(0, False), spec(1, True), spec(1, False), spec(2, True), spec(2, False)],
        out_specs=[ospec, ospec],
        compiler_params=_cparams(("parallel", "parallel")),
        name="dilated_%d" % gi,
    )(view, view, view, view, view)
    return o.reshape(s, DIL_BLOCK_W), lse.reshape(s, DIL_BLOCK_W)


def _merge_body(x_ref, g_ref, ya_ref, oct_ref, ost_ref, owt_ref, glt_ref,
                o0_ref, l0_ref, o1_ref, l1_ref, o2_ref, l2_ref,
                wm_ref, wua_ref, wub_ref, wuc_ref, wo_ref, out_ref):
    x = x_ref[...]
    h = _rms(x, g_ref[...]).astype(BF16)
    gates = jax.nn.sigmoid(_dot(h, wm_ref[...]))
    y_a = _dot(ya_ref[...], wua_ref[...])

    bg = jax.nn.sigmoid(glt_ref[...])
    pieces = []
    for hd in range(NSA_HEADS):
        rows = slice(HEAD_DIM * hd, HEAD_DIM * (hd + 1))
        pieces.append(bg[3 * hd:3 * hd + 1, :] * oct_ref[rows, :]
                      + bg[3 * hd + 1:3 * hd + 2, :] * ost_ref[rows, :]
                      + bg[3 * hd + 2:3 * hd + 3, :] * owt_ref[rows, :])
    yb_t = jnp.concatenate(pieces, axis=0)
    y_b = _dot(yb_t.T.astype(BF16), wub_ref[...])

    l0, l1, l2 = l0_ref[...], l1_ref[...], l2_ref[...]
    mx = jnp.maximum(jnp.maximum(l0, l1), l2)
    e0, e1, e2 = jnp.exp(l0 - mx), jnp.exp(l1 - mx), jnp.exp(l2 - mx)
    inv = 1.0 / (e0 + e1 + e2)
    yc = (e0 * inv) * o0_ref[...] + (e1 * inv) * o1_ref[...] + (e2 * inv) * o2_ref[...]
    y_c = _dot(yc.astype(BF16), wuc_ref[...])

    d = x.shape[1]
    merged = gates[:, 0:d] * y_a + gates[:, d:2 * d] * y_b + gates[:, 2 * d:3 * d] * y_c
    out_ref[...] = x + _dot(merged.astype(BF16), wo_ref[...])


def _merge(x, g, ya, oct_, ost, owt, glt, dil, w_in, w_up_a, w_up_b, w_up_c, w_out):
    s, d = x.shape
    tm = min(MERGE_ROWS, s)
    w_m = w_in[:, IN_OFFSETS[12]:].astype(BF16)
    wuc = w_up_c.reshape(DIL_HEADS_PER_GROUP, HEAD_DIM, d)
    wuc = jnp.pad(wuc, ((0, 0), (0, LANES - HEAD_DIM), (0, 0))).reshape(DIL_BLOCK_W, d).astype(BF16)
    weights = [w_m, w_up_a.astype(BF16), w_up_b.astype(BF16), wuc, w_out.astype(BF16)]

    def rows(n):
        return pl.BlockSpec((tm, n), lambda i: (i, 0))

    def cols(n):
        return pl.BlockSpec((n, tm), lambda i: (0, i))

    dil_args, dil_specs = [], []
    for o, lse in dil:
        dil_args += [o, lse]
        dil_specs += [rows(DIL_BLOCK_W), rows(DIL_BLOCK_W)]
    return pl.pallas_call(
        _merge_body,
        out_shape=jax.ShapeDtypeStruct((s, d), F32),
        grid=(s // tm,),
        in_specs=[rows(d), _resident((1, d)), rows(LRU_WIDTH), cols(NSA_Q_W), cols(NSA_Q_W),
                  cols(NSA_Q_W), cols(GATE_ROWS)] + dil_specs + [_resident(w.shape) for w in weights],
        out_specs=rows(d),
        compiler_params=_cparams(("parallel",)),
        name="merge",
    )(x, g.reshape(1, d), ya, oct_, ost, owt, glt, *dil_args, *weights)


def _mixer(x, mix_norm, w_in, conv_w, conv_b, lru_wa, lru_ba, lru_wi, lru_bi, lru_lambda,
           cmp_pos_k, cmp_pos_v, cmp_k_w1, cmp_k_w2, cmp_v_w1, cmp_v_w2,
           w_up_a, w_up_b, w_up_c, w_out):
    a2, qt, kvc, ksw, vswt, glt, *c = _proj(x, mix_norm, w_in)
    ya = _lru(a2, conv_w, conv_b, lru_wa, lru_ba, lru_wi, lru_bi, lru_lambda)
    kc = _compress(kvc[:, :NSA_KV_W], cmp_pos_k, cmp_k_w1, cmp_k_w2, transposed=False)
    vct = _compress(kvc[:, NSA_KV_W:], cmp_pos_v, cmp_v_w1, cmp_v_w2, transposed=True)
    oct_, sel = _cmp_topk(qt, kc, vct)
    ost = _band(qt, ksw, vswt, sel, mode="sel")
    owt = _band(qt, ksw, vswt, None, mode="win")
    dil = [_dilated_group(c[gi], gi) for gi in range(len(DIL_GROUPS))]
    return _merge(x, mix_norm, ya, oct_, ost, owt, glt, dil, w_in, w_up_a, w_up_b, w_up_c, w_out)


def kernel(x, ffn1_norm, ffn1_w1, ffn1_w3, ffn1_w2, mix_norm, w_in, conv_w, conv_b, lru_wa, lru_ba, lru_wi, lru_bi, lru_lambda, cmp_pos_k, cmp_pos_v, cmp_k_w1, cmp_k_w2, cmp_v_w1, cmp_v_w2, w_up_a, w_up_b, w_up_c, w_out, ffn2_norm, ffn2_w1, ffn2_w3, ffn2_w2, final_norm):
    batch, s, d = x.shape
    assert batch == 1 and d == D_MODEL
    depth = w_in.shape[0]
    y = x.reshape(s, d)
    for l in range(depth):
        y = _ffn(y, ffn1_norm[l], ffn1_w1[l], ffn1_w3[l], ffn1_w2[l], final_norm, final=False)
        y = _mixer(y, mix_norm[l], w_in[l], conv_w[l], conv_b[l], lru_wa[l], lru_ba[l], lru_wi[l],
                   lru_bi[l], lru_lambda[l], cmp_pos_k[l], cmp_pos_v[l], cmp_k_w1[l], cmp_k_w2[l],
                   cmp_v_w1[l], cmp_v_w2[l], w_up_a[l], w_up_b[l], w_up_c[l], w_out[l])
        y = _ffn(y, ffn2_norm[l], ffn2_w1[l], ffn2_w3[l], ffn2_w2[l], final_norm,
                 final=(l == depth - 1))
    return y.reshape(batch, s, d)
```

```python
import functools

import jax
import jax.numpy as jnp
import numpy as np
from jax import lax
from jax.experimental import pallas as pl
from jax.experimental.pallas import tpu as pltpu

F32 = jnp.float32
BF16 = jnp.bfloat16

D_MODEL = 1024
D_FF = 2816
HEAD_DIM = 64
RMS_EPS = 1e-6
NEG_INF = -1e30
SCALE = HEAD_DIM ** -0.5
QK_SCALE_LOG2 = float(SCALE * np.log2(np.e))

LRU_WIDTH = 768
CONV_WIDTH = 4
LRU_C = 8.0

NSA_HEADS = 12
NSA_KV_HEADS = 3
NSA_GROUP = 4
NSA_Q_W = 768
NSA_KV_W = 192
CMP_BLOCK = 32
CMP_STRIDE = 16
SEL_BLOCK = 64
SEL_TOPN = 16
WIN = 512
FORCE_SCORE = 1e9

DIL_GROUPS = ((128, 1), (512, 4), (2048, 16))
DIL_HEADS_PER_GROUP = 4
DIL_W = 768

IN_SPLITS = (768, 768, 768, 192, 192, 192, 192, 192, 192, 36, 768, 768, 768, 1024, 1024, 1024)
IN_OFFSETS = tuple(int(o) for o in np.cumsum(IN_SPLITS)[:-1])

LANES = 128
V7X_VMEM_LIMIT_BYTES = 56 * 1024 * 1024

FF_CHUNK = 256
FFN_ROWS = 1024
PROJ_ROWS = 256
LRU_ROWS = 256
MERGE_ROWS = 256
Q_BLOCK = 128
KEY_CHUNK = 256
BAND_Q = 256
SUM_ROWS = 16
BAND_UNROLL = 4
GATE_ROWS = 40
CMP_TILE = 64
DIL_BLOCK_W = DIL_HEADS_PER_GROUP * LANES
DIL_GROUP_W = 3 * DIL_BLOCK_W


def _cparams(sem):
    return pltpu.CompilerParams(dimension_semantics=sem, vmem_limit_bytes=V7X_VMEM_LIMIT_BYTES)


def _resident(shape):
    nd = len(shape)
    return pl.BlockSpec(shape, lambda *_: (0,) * nd, pipeline_mode=pl.Buffered(1))


def _rms(x, g):
    return x * lax.rsqrt(jnp.mean(x * x, axis=-1, keepdims=True) + RMS_EPS) * g


def _gelu_tanh(x):
    c = np.float32(np.sqrt(2.0 / np.pi))
    return x * (0.5 * (1.0 + jnp.tanh(c * (x + 0.044715 * (x * x * x)))))


def _dot(a, b):
    return jnp.dot(a, b, preferred_element_type=F32)


def _dot_nt(a, b):
    return lax.dot_general(a, b, (((1,), (1,)), ((), ())), preferred_element_type=F32)


def _ffn_body(x_ref, g_ref, w1_ref, w3_ref, w2_ref, fg_ref, o_ref, h_ref, acc_ref, *, final):
    x = x_ref[...]
    h_ref[...] = _rms(x, g_ref[...]).astype(BF16)
    acc_ref[...] = jnp.zeros_like(acc_ref)

    def chunk(c, carry):
        h = h_ref[...]
        a = _dot(h, w1_ref[c])
        b = _dot(h, w3_ref[c])
        gated = (a * jax.nn.sigmoid(a) * b).astype(BF16)
        acc_ref[...] += _dot(gated, w2_ref[c])
        return carry

    lax.fori_loop(0, w1_ref.shape[0], chunk, 0)
    y = x + 0.5 * acc_ref[...]
    if final:
        y = _rms(y, fg_ref[...])
    o_ref[...] = y


def _ffn(x, g, w1, w3, w2, fg, *, final):
    s, d = x.shape
    nc = D_FF // FF_CHUNK
    w1c = w1.astype(BF16).reshape(d, nc, FF_CHUNK).transpose(1, 0, 2)
    w3c = w3.astype(BF16).reshape(d, nc, FF_CHUNK).transpose(1, 0, 2)
    w2c = w2.astype(BF16).reshape(nc, FF_CHUNK, d)
    tm = min(FFN_ROWS, s)
    row = pl.BlockSpec((tm, d), lambda i: (i, 0))
    return pl.pallas_call(
        functools.partial(_ffn_body, final=final),
        out_shape=jax.ShapeDtypeStruct((s, d), F32),
        grid=(s // tm,),
        in_specs=[row, _resident((1, d)), _resident(w1c.shape), _resident(w3c.shape),
                  _resident(w2c.shape), _resident((1, d))],
        out_specs=row,
        scratch_shapes=[pltpu.VMEM((tm, d), BF16), pltpu.VMEM((tm, d), F32)],
        compiler_params=_cparams(("parallel",)),
        name="ffn_final" if final else "ffn",
    )(x, g.reshape(1, d), w1c, w3c, w2c, fg.reshape(1, d))


def _proj_body(x_ref, g_ref, w_a_ref, w_q_ref, w_kvc_ref, w_ksw_ref, w_vsw_ref, w_gl_ref,
               w_c0_ref, w_c1_ref, w_c2_ref,
               a2_ref, qt_ref, kvc_ref, ksw_ref, vswt_ref, glt_ref, c0_ref, c1_ref, c2_ref, slab_s):
    tm = x_ref.shape[0]
    h = _rms(x_ref[...], g_ref[...]).astype(BF16)
    a2_ref[...] = _dot_nt(h, w_a_ref[...])
    qt_ref[...] = (_dot_nt(w_q_ref[...], h) * QK_SCALE_LOG2).astype(BF16)
    kvc_ref[...] = _dot_nt(h, w_kvc_ref[...])
    ksw_ref[...] = _dot_nt(h, w_ksw_ref[...]).astype(BF16)
    vswt_ref[0] = _dot_nt(w_vsw_ref[...], h).astype(BF16)
    glt_ref[...] = _dot_nt(w_gl_ref[...], h)
    for w_ref, c_ref, (_, dil) in zip((w_c0_ref, w_c1_ref, w_c2_ref), (c0_ref, c1_ref, c2_ref), DIL_GROUPS):
        for part, scale in ((slice(0, DIL_BLOCK_W), SCALE), (slice(DIL_BLOCK_W, DIL_GROUP_W), 1.0)):
            res = _dot_nt(h, w_ref[part, :])
            if scale != 1.0:
                res = res * scale
            if dil == 1:
                c_ref[:, part] = res.astype(BF16)
                continue
            for sl in range(part.start // LANES, part.stop // LANES):
                slab_s[sl] = res[:, LANES * sl - part.start:LANES * (sl + 1) - part.start]
                for r in range(dil):
                    col = r * DIL_GROUP_W + LANES * sl
                    c_ref[:, col:col + LANES] = slab_s[sl, pl.ds(r, tm // dil, stride=dil), :].astype(BF16)


def _pad_heads(w, n_heads):
    k = w.shape[1]
    w = w.reshape(n_heads, HEAD_DIM, k)
    w = jnp.pad(w, ((0, 0), (0, LANES - HEAD_DIM), (0, 0)))
    return w.reshape(n_heads * LANES, k)


def _proj(x, g, w_in_t):
    s, d = x.shape
    (a_x, a_gate, b_q, b_kc, b_vc, b_ks, b_vs, b_kw, b_vw, b_gate,
     c_q, c_k, c_v, _, _, _) = jnp.split(w_in_t, list(IN_OFFSETS), axis=0)
    w_a = jnp.concatenate([a_x, a_gate], axis=0)
    w_kvc = jnp.concatenate([b_kc, b_vc], axis=0)
    w_ksw = jnp.concatenate([_pad_heads(b_ks, 3), _pad_heads(b_kw, 3)], axis=0)
    w_vsw = jnp.concatenate([b_vs, b_vw], axis=0)
    w_gl = jnp.pad(b_gate, ((0, GATE_ROWS - b_gate.shape[0]), (0, 0)))
    gw = DIL_HEADS_PER_GROUP * HEAD_DIM
    w_c = [jnp.concatenate([_pad_heads(t[gi * gw:(gi + 1) * gw], DIL_HEADS_PER_GROUP)
                            for t in (c_q, c_k, c_v)], axis=0) for gi in range(len(DIL_GROUPS))]
    weights = [w.astype(BF16) for w in [w_a, b_q, w_kvc, w_ksw, w_vsw, w_gl] + w_c]
    tm = PROJ_ROWS
    assert tm == KEY_CHUNK and s % tm == 0

    def rows(n):
        return pl.BlockSpec((tm, n), lambda i: (i, 0))

    def cols(n):
        return pl.BlockSpec((n, tm), lambda i: (0, i))

    out_shape = [
        jax.ShapeDtypeStruct((s, 2 * LRU_WIDTH), F32),
        jax.ShapeDtypeStruct((NSA_Q_W, s), BF16),
        jax.ShapeDtypeStruct((s, 2 * NSA_KV_W), F32),
        jax.ShapeDtypeStruct((s, 6 * LANES), BF16),
        jax.ShapeDtypeStruct((s // tm, 2 * NSA_KV_W, tm), BF16),
        jax.ShapeDtypeStruct((GATE_ROWS, s), F32),
    ]
    out_specs = [rows(2 * LRU_WIDTH), cols(NSA_Q_W), rows(2 * NSA_KV_W), rows(6 * LANES),
                 pl.BlockSpec((1, 2 * NSA_KV_W, tm), lambda i: (i, 0, 0)), cols(GATE_ROWS)]
    for _, dil in DIL_GROUPS:
        out_shape.append(jax.ShapeDtypeStruct((s // dil, dil * DIL_GROUP_W), BF16))
        out_specs.append(pl.BlockSpec((tm // dil, dil * DIL_GROUP_W), lambda i: (i, 0)))
    return pl.pallas_call(
        _proj_body,
        out_shape=out_shape,
        grid=(s // tm,),
        in_specs=[rows(d), _resident((1, d))] + [_resident(w.shape) for w in weights],
        out_specs=out_specs,
        scratch_shapes=[pltpu.VMEM((DIL_GROUP_W // LANES, tm, LANES), F32)],
        compiler_params=_cparams(("parallel",)),
        name="proj",
    )(x, g.reshape(1, d), *weights)


def _lru_body(a2_ref, cw_ref, cb_ref, wa_ref, ba_ref, wi_ref, bi_ref, lam_ref, y_ref,
              xbuf, a_s, u_s, h_s, hc):
    t_rows = y_ref.shape[0]
    w = LRU_WIDTH

    @pl.when(pl.program_id(0) == 0)
    def _():
        xbuf[0:8, :] = jnp.zeros((8, w), F32)
        hc[...] = jnp.zeros_like(hc)

    x = a2_ref[:, 0:w]
    xbuf[8:8 + t_rows, :] = x
    cw = cw_ref[...]
    xc = cb_ref[...] + cw[0:1] * xbuf[5:5 + t_rows, :]
    xc = xc + cw[1:2] * xbuf[6:6 + t_rows, :]
    xc = xc + cw[2:3] * xbuf[7:7 + t_rows, :]
    xc = xc + cw[3:4] * x
    xbuf[0:8, :] = x[t_rows - 8:t_rows, :]

    xcb = xc.astype(BF16)
    ra, ri = [], []
    for p in range(wa_ref.shape[0]):
        blk = xcb[:, 256 * p:256 * (p + 1)]
        ra.append(_dot(blk, wa_ref[p]))
        ri.append(_dot(blk, wi_ref[p]))
    r = jax.nn.sigmoid(jnp.concatenate(ra, axis=1) + ba_ref[...])
    ig = jax.nn.sigmoid(jnp.concatenate(ri, axis=1) + bi_ref[...])
    z = -lam_ref[...]
    softplus = jnp.maximum(z, 0.0) + jnp.log1p(jnp.exp(-jnp.abs(z)))
    log_a = (-LRU_C * r) * softplus
    a_s[...] = jnp.exp(log_a)
    u_s[...] = jnp.sqrt(1.0 - jnp.exp(2.0 * log_a)) * (ig * xc)

    row = lax.broadcasted_iota(jnp.int32, (8, w), 0)

    def group(gidx, hprev):
        base = pl.multiple_of(gidx * 8, 8)
        a = a_s[pl.ds(base, 8), :]
        b = u_s[pl.ds(base, 8), :]
        for dshift in (1, 2, 4):
            ok = row >= dshift
            a_sh = pltpu.roll(a, dshift, 0)
            b_sh = pltpu.roll(b, dshift, 0)
            b = jnp.where(ok, a * b_sh + b, b)
            a = jnp.where(ok, a * a_sh, a)
        h8 = a * hprev + b
        h_s[pl.ds(base, 8), :] = h8
        return jnp.broadcast_to(h8[7:8, :], (8, w))

    hc[...] = lax.fori_loop(0, t_rows // 8, group, hc[...])
    y_ref[...] = (h_s[...] * _gelu_tanh(a2_ref[:, w:2 * w])).astype(BF16)


def _pair_block_diag(wb):
    z = jnp.zeros((128, 128), wb.dtype)
    return jnp.stack([jnp.block([[wb[2 * p], z], [z, wb[2 * p + 1]]]) for p in range(3)])


def _lru(a2, conv_w, conv_b, wa, ba, wi, bi, lam):
    s = a2.shape[0]
    w = LRU_WIDTH
    t = min(LRU_ROWS, s)
    wa2 = _pair_block_diag(wa.astype(BF16))
    wi2 = _pair_block_diag(wi.astype(BF16))
    vec = _resident((1, w))
    return pl.pallas_call(
        _lru_body,
        out_shape=jax.ShapeDtypeStruct((s, w), BF16),
        grid=(s // t,),
        in_specs=[pl.BlockSpec((t, 2 * w), lambda i: (i, 0)), _resident((CONV_WIDTH, w)), vec,
                  _resident(wa2.shape), vec, _resident(wi2.shape), vec, vec],
        out_specs=pl.BlockSpec((t, w), lambda i: (i, 0)),
        scratch_shapes=[pltpu.VMEM((t + 8, w), F32), pltpu.VMEM((t, w), F32), pltpu.VMEM((t, w), F32),
                        pltpu.VMEM((t, w), F32), pltpu.VMEM((8, w), F32)],
        compiler_params=_cparams(("arbitrary",)),
        name="lru",
    )(a2, conv_w, conv_b.reshape(1, w), wa2, ba.reshape(1, w), wi2, bi.reshape(1, w), lam.reshape(1, w))


CMP_HALF = CMP_STRIDE * NSA_KV_W


def _compress_body(x_ref, pos_ref, w1_ref, w2_ref, o_ref, *, transposed):
    hids = []
    for j in range(4):
        xa = (x_ref[:, CMP_HALF * j:CMP_HALF * (j + 1)] + pos_ref[:, :CMP_HALF]).astype(BF16)
        xb = (x_ref[:, CMP_HALF * (j + 1):CMP_HALF * (j + 2)] + pos_ref[:, CMP_HALF:]).astype(BF16)
        pre = _dot(xa, w1_ref[:CMP_HALF, :]) + _dot(xb, w1_ref[CMP_HALF:, :])
        hids.append(_gelu_tanh(pre).astype(BF16))
    hid = jnp.concatenate(hids, axis=0)
    if transposed:
        o_ref[...] = _dot_nt(w2_ref[...], hid).astype(BF16)
    else:
        o_ref[...] = _dot(hid, w2_ref[...]).astype(BF16)


def _compress(kv, pos, w1, w2, *, transposed):
    s = kv.shape[0]
    n_sel = s // SEL_BLOCK
    assert n_sel % CMP_TILE == 0
    rows = kv.reshape(n_sel, SEL_BLOCK * NSA_KV_W)
    nxt = jnp.concatenate([rows[1:, :CMP_HALF], jnp.zeros((1, CMP_HALF), kv.dtype)], axis=0)
    ext = jnp.concatenate([rows, nxt], axis=1)
    eye = jnp.eye(NSA_KV_HEADS, dtype=F32)
    w1e = jnp.einsum("ldn,hg->lhdgn", w1.reshape(CMP_BLOCK, HEAD_DIM, -1), eye)
    w1e = w1e.reshape(CMP_BLOCK * NSA_KV_W, -1).astype(BF16)
    hid = w1.shape[1]
    if transposed:
        w2e = jnp.einsum("nd,hg->gdhn", w2, eye).reshape(NSA_KV_W, NSA_KV_HEADS * hid).astype(BF16)
        out_shape = jax.ShapeDtypeStruct((NSA_KV_W, 4 * n_sel), BF16)
        out_spec = pl.BlockSpec((NSA_KV_W, 4 * CMP_TILE), lambda tt: (0, tt))
    else:
        w2p = jnp.pad(w2, ((0, 0), (0, LANES - HEAD_DIM)))
        w2e = jnp.einsum("nd,hg->hngd", w2p, eye).reshape(NSA_KV_HEADS * hid, NSA_KV_HEADS * LANES).astype(BF16)
        out_shape = jax.ShapeDtypeStruct((4 * n_sel, NSA_KV_HEADS * LANES), BF16)
        out_spec = pl.BlockSpec((4 * CMP_TILE, NSA_KV_HEADS * LANES), lambda tt: (tt, 0))
    pose = jnp.broadcast_to(pos[:, None, :], (CMP_BLOCK, NSA_KV_HEADS, HEAD_DIM)).reshape(1, -1)
    return pl.pallas_call(
        functools.partial(_compress_body, transposed=transposed),
        out_shape=out_shape,
        grid=(n_sel // CMP_TILE,),
        in_specs=[pl.BlockSpec((CMP_TILE, ext.shape[1]), lambda tt: (tt, 0)),
                  _resident(pose.shape), _resident(w1e.shape), _resident(w2e.shape)],
        out_specs=out_spec,
        compiler_params=_cparams(("parallel",)),
        name="compress_v" if transposed else "compress_k",
    )(ext, pose, w1e, w2e)


def _cmp_topk_tiles(qt_ref, kc_ref, vct_ref, oct_ref, bias_ref, *, n_tiles, n_sel, n_top):
    i = pl.program_id(0)
    n_rows = 4 * CMP_TILE * n_tiles
    n_blk = CMP_TILE * n_tiles
    lane_t = i * Q_BLOCK + lax.broadcasted_iota(jnp.int32, (1, Q_BLOCK), 1)
    r = lax.broadcasted_iota(jnp.int32, (n_rows, Q_BLOCK), 0)
    n_of_row = CMP_TILE * (r // (4 * CMP_TILE)) + r % CMP_TILE
    j_of_row = (r % (4 * CMP_TILE)) // CMP_TILE
    valid = SEL_BLOCK * n_of_row + CMP_STRIDE * j_of_row + (CMP_BLOCK - 1) <= lane_t
    has_valid = lane_t >= CMP_BLOCK - 1
    blk = lax.broadcasted_iota(jnp.int32, (n_blk, Q_BLOCK), 0)
    blk_f = blk.astype(F32)
    cur = lane_t // SEL_BLOCK
    forced = (blk == cur) | (blk == 0)
    future = blk > cur

    scores = []
    for k in range(NSA_KV_HEADS):
        kc = kc_ref[0:n_rows, LANES * k:LANES * k + HEAD_DIM]
        vct = jnp.concatenate([vct_ref[HEAD_DIM * k:HEAD_DIM * (k + 1), 0:n_rows],
                               jnp.ones((SUM_ROWS, n_rows), BF16)], axis=0)
        imp = jnp.zeros((n_blk, Q_BLOCK), F32)
        raw = [_dot(kc, qt_ref[HEAD_DIM * (NSA_GROUP * k + g):HEAD_DIM * (NSA_GROUP * k + g + 1), :])
               for g in range(NSA_GROUP)]
        for g in range(NSA_GROUP):
            h = NSA_GROUP * k + g
            s = jnp.where(valid, raw[g], NEG_INF)
            m = jnp.max(s, axis=0, keepdims=True)
            e = jnp.exp2(s - m)
            pv = _dot(vct, e.astype(BF16))
            den = jnp.maximum(pv[HEAD_DIM:HEAD_DIM + 1, :], 1e-30)
            inv = jnp.where(has_valid, 1.0 / den, 0.0)
            oct_ref[HEAD_DIM * h:HEAD_DIM * (h + 1), :] = pv[0:HEAD_DIM, :] * inv
            parts = []
            for tt in range(n_tiles):
                base = 4 * CMP_TILE * tt
                acc = e[base:base + CMP_TILE, :]
                for j in range(1, 4):
                    acc = acc + e[base + j * CMP_TILE:base + (j + 1) * CMP_TILE, :]
                parts.append(acc)
            imp = imp + jnp.concatenate(parts, axis=0) * inv
        scores.append(jnp.where(forced, FORCE_SCORE, jnp.where(future, -1.0, imp)))

    def pick_one(_, carry):
        out = []
        for v in carry:
            top = jnp.max(v, axis=0, keepdims=True)
            first = jnp.min(jnp.where(v == top, blk_f, float(n_blk)), axis=0, keepdims=True)
            out.append(jnp.where(blk_f == first, -jnp.inf, v))
        return tuple(out)

    res = lax.fori_loop(0, n_top, pick_one, tuple(scores))
    for k in range(NSA_KV_HEADS):
        bias_ref[k, 0:n_blk, :] = jnp.where(res[k] == -jnp.inf, 0.0, NEG_INF)
        if n_blk < n_sel:
            bias_ref[k, n_blk:n_sel, :] = jnp.full((n_sel - n_blk, Q_BLOCK), NEG_INF, F32)


def _cmp_topk_body(qt_ref, kc_ref, vct_ref, oct_ref, bias_ref, *, n_sel, n_top):
    total = n_sel // CMP_TILE
    need = jnp.minimum((2 * pl.program_id(0) + 1) // CMP_TILE + 1, total)
    for n_tiles in range(1, total + 1):
        pl.when(need == n_tiles)(functools.partial(
            _cmp_topk_tiles, qt_ref, kc_ref, vct_ref, oct_ref, bias_ref,
            n_tiles=n_tiles, n_sel=n_sel, n_top=n_top))


def _cmp_topk(qt, kc, vct):
    s = qt.shape[1]
    n_sel = s // SEL_BLOCK
    n_top = min(SEL_TOPN, n_sel)
    qspec = pl.BlockSpec((NSA_Q_W, Q_BLOCK), lambda i: (0, i))
    return pl.pallas_call(
        functools.partial(_cmp_topk_body, n_sel=n_sel, n_top=n_top),
        out_shape=[jax.ShapeDtypeStruct((NSA_Q_W, s), F32),
                   jax.ShapeDtypeStruct((NSA_KV_HEADS, n_sel, s), F32)],
        grid=(s // Q_BLOCK,),
        in_specs=[qspec, _resident(kc.shape), _resident(vct.shape)],
        out_specs=[qspec, pl.BlockSpec((NSA_KV_HEADS, n_sel, Q_BLOCK), lambda i: (0, 0, i))],
        compiler_params=_cparams(("parallel",)),
        name="cmp_topk",
    )(qt, kc, vct)


def _band_body(*refs, mode, k_off, v_off):
    if mode == "sel":
        qt_ref, k_ref, vt_ref, bias_ref, o_ref, acc_s, ml_s, s_s = refs
    else:
        qt_ref, k_ref, vt_ref, o_ref, acc_s, ml_s, s_s = refs
        bias_ref = None
    i = pl.program_id(0)
    wide = NSA_GROUP * BAND_Q
    blocks = KEY_CHUNK // SEL_BLOCK
    lane = lax.broadcasted_iota(jnp.int32, (1, wide), 1)
    t = i * BAND_Q + lane % BAND_Q
    krow = lax.broadcasted_iota(jnp.int32, (KEY_CHUNK, wide), 0)
    c_hi = (i * BAND_Q + (BAND_Q - 1)) // KEY_CHUNK + 1

    acc_s[...] = jnp.zeros_like(acc_s)
    ml_s[...] = jnp.full(ml_s.shape, NEG_INF, F32)
    ones = jnp.ones((SUM_ROWS, KEY_CHUNK), BF16)

    def scores(c, k):
        base = pl.multiple_of(c * KEY_CHUNK, KEY_CHUNK)
        q4 = jnp.concatenate(
            [qt_ref[HEAD_DIM * (NSA_GROUP * k + g):HEAD_DIM * (NSA_GROUP * k + g + 1), :]
             for g in range(NSA_GROUP)], axis=1)
        keys = k_ref[pl.ds(base, KEY_CHUNK), k_off + LANES * k:k_off + LANES * k + HEAD_DIM]
        s_s[k] = _dot(keys, q4)

    def softmax(c, k, masked):
        base = pl.multiple_of(c * KEY_CHUNK, KEY_CHUNK)
        s = s_s[k]
        if mode == "sel":
            rows = []
            for rblk in range(blocks):
                b = bias_ref[k, pl.ds(c * blocks + rblk, 1), :]
                b = jnp.concatenate([b] * NSA_GROUP, axis=1)
                rows.append(jnp.broadcast_to(b, (SEL_BLOCK, wide)))
            s = s + jnp.concatenate(rows, axis=0)
            if masked:
                s = jnp.where(base + krow <= t, s, NEG_INF)
        else:
            dist = t - (base + krow)
            s = jnp.where((dist >= 0) & (dist < WIN), s, NEG_INF)
        m_old = ml_s[k:k + 1, :]
        m_new = jnp.maximum(m_old, jnp.max(s, axis=0, keepdims=True))
        ml_s[k:k + 1, :] = m_new
        return jnp.exp2(m_old - m_new), jnp.exp2(s - m_new).astype(BF16)

    def accumulate(c, k, alpha, e):
        vt = jnp.concatenate([vt_ref[c, v_off + HEAD_DIM * k:v_off + HEAD_DIM * (k + 1), :], ones], axis=0)
        acc_s[k] = alpha * acc_s[k] + _dot(vt, e)

    def step(c, masked, prefetch):
        for k in range(NSA_KV_HEADS):
            alpha, e = softmax(c, k, masked)
            accumulate(c, k, alpha, e)
            if prefetch:
                scores(c + 1, k)

    if mode == "sel":
        c_lo = 0
    else:
        c_lo = jnp.maximum(i * BAND_Q - (WIN - 1), 0) // KEY_CHUNK
    for k in range(NSA_KV_HEADS):
        scores(c_lo, k)
    early_masked = mode == "win"
    n_early = c_hi - 1 - c_lo

    def group(p, carry):
        for j in range(BAND_UNROLL):
            step(c_lo + BAND_UNROLL * p + j, early_masked, True)
        return carry

    n_groups = n_early // BAND_UNROLL
    lax.fori_loop(0, n_groups, group, 0)
    rest_lo = c_lo + BAND_UNROLL * n_groups
    for j in range(BAND_UNROLL - 1):
        pl.when(rest_lo + j < c_hi - 1)(functools.partial(step, rest_lo + j, early_masked, True))
    step(c_hi - 1, True, False)

    for k in range(NSA_KV_HEADS):
        den = jnp.maximum(acc_s[k, HEAD_DIM:HEAD_DIM + 1, :], 1e-30)
        out = acc_s[k, 0:HEAD_DIM, :] * (1.0 / den)
        for g in range(NSA_GROUP):
            h = NSA_GROUP * k + g
            o_ref[HEAD_DIM * h:HEAD_DIM * (h + 1), :] = out[:, BAND_Q * g:BAND_Q * (g + 1)]


def _band(qt, ksw, vswt, sel, *, mode):
    s = qt.shape[1]
    assert (BAND_Q % KEY_CHUNK == 0 or KEY_CHUNK % BAND_Q == 0) and s % BAND_Q == 0
    qspec = pl.BlockSpec((NSA_Q_W, BAND_Q), lambda i: (0, i))
    k_off = 0 if mode == "sel" else NSA_KV_HEADS * LANES
    v_off = 0 if mode == "sel" else NSA_KV_W
    in_specs = [qspec, _resident(ksw.shape), _resident(vswt.shape)]
    args = [qt, ksw, vswt]
    if mode == "sel":
        in_specs.append(pl.BlockSpec((NSA_KV_HEADS, sel.shape[1], BAND_Q), lambda i: (0, 0, i)))
        args.append(sel)
    wide = NSA_GROUP * BAND_Q
    return pl.pallas_call(
        functools.partial(_band_body, mode=mode, k_off=k_off, v_off=v_off),
        out_shape=jax.ShapeDtypeStruct((NSA_Q_W, s), F32),
        grid=(s // BAND_Q,),
        in_specs=in_specs,
        out_specs=qspec,
        scratch_shapes=[pltpu.VMEM((NSA_KV_HEADS, HEAD_DIM + SUM_ROWS, wide), F32), pltpu.VMEM((8, wide), F32),
                        pltpu.VMEM((NSA_KV_HEADS, KEY_CHUNK, wide), F32)],
        compiler_params=_cparams(("parallel",)),
        name="nsa_" + mode,
    )(*args)


def _dil_body(q_ref, kp_ref, kc_ref, vp_ref, vc_ref, o_ref, lse_ref, *, span):
    i = pl.program_id(1)
    rq = lax.broadcasted_iota(jnp.int32, (Q_BLOCK, 2 * Q_BLOCK), 0)
    ck = lax.broadcasted_iota(jnp.int32, (Q_BLOCK, 2 * Q_BLOCK), 1)
    delta = Q_BLOCK + rq - ck
    key_idx = (i - 1) * Q_BLOCK + ck
    valid = (delta >= 0) & (delta <= span) & (key_idx >= 0)
    raw = []
    for j in range(DIL_HEADS_PER_GROUP):
        q = q_ref[:, LANES * j:LANES * j + HEAD_DIM]
        kk = jnp.concatenate([kp_ref[:, LANES * j:LANES * j + HEAD_DIM],
                              kc_ref[:, LANES * j:LANES * j + HEAD_DIM]], axis=0)
        raw.append(_dot_nt(q, kk))
    for j in range(DIL_HEADS_PER_GROUP):
        vv = jnp.concatenate([vp_ref[:, LANES * j:LANES * (j + 1)],
                              vc_ref[:, LANES * j:LANES * (j + 1)]], axis=0)
        s = jnp.where(valid, raw[j], NEG_INF)
        m = jnp.max(s, axis=1, keepdims=True)
        e = jnp.where(valid, jnp.exp(s - m), 0.0)
        den = jnp.maximum(jnp.sum(e, axis=1, keepdims=True), 1e-30)
        p = e * (1.0 / den)
        o_ref[:, LANES * j:LANES * (j + 1)] = _dot(p.astype(BF16), vv)
        lse_ref[:, LANES * j:LANES * (j + 1)] = jnp.broadcast_to(m + jnp.log(den), (Q_BLOCK, LANES))


def _dilated_group(c, gi):
    window, dil = DIL_GROUPS[gi]
    span = window // dil
    assert span == Q_BLOCK
    view = c
    rows = view.shape[0]
    s = rows * dil
    spec = lambda tensor, prev: pl.BlockSpec(
        (Q_BLOCK, DIL_BLOCK_W),
        lambda r, i: (jnp.maximum(i - 1, 0) if prev else i, 3 * r + tensor))
    ospec = pl.BlockSpec((Q_BLOCK, DIL_BLOCK_W), lambda r, i: (i, r))
    o, lse = pl.pallas_call(
        functools.partial(_dil_body, span=span),
        out_shape=[jax.ShapeDtypeStruct((rows, dil * DIL_BLOCK_W), F32)] * 2,
        grid=(dil, rows // Q_BLOCK),
        in_specs=[spec(0, False), spec(1, True), spec(1, False), spec(2, True), spec(2, False)],
        out_specs=[ospec, ospec],
        compiler_params=_cparams(("parallel", "parallel")),
        name="dilated_%d" % gi,
    )(view, view, view, view, view)
    return o, lse


def _merge_body(x_ref, g_ref, ya_ref, oct_ref, ost_ref, owt_ref, glt_ref,
                o0_ref, l0_ref, o1_ref, l1_ref, o2_ref, l2_ref,
                wm_ref, wua_ref, wub_ref, wuc_ref, wo_ref, out_ref, nat_s):
    tm = x_ref.shape[0]

    def natural(view_ref, dil, slot):
        if dil == 1:
            return view_ref[...]
        for r in range(dil):
            for sl in range(DIL_BLOCK_W // LANES):
                col = r * DIL_BLOCK_W + LANES * sl
                nat_s[slot, sl, pl.ds(r, tm // dil, stride=dil), :] = view_ref[:, col:col + LANES]
        return jnp.concatenate([nat_s[slot, sl] for sl in range(DIL_BLOCK_W // LANES)], axis=1)

    dils = [dil for _, dil in DIL_GROUPS]
    o0, o1, o2 = (natural(r, dl, 2 * n) for n, (r, dl) in enumerate(zip((o0_ref, o1_ref, o2_ref), dils)))
    l0, l1, l2 = (natural(r, dl, 2 * n + 1) for n, (r, dl) in enumerate(zip((l0_ref, l1_ref, l2_ref), dils)))
    x = x_ref[...]
    h = _rms(x, g_ref[...]).astype(BF16)
    gates = jax.nn.sigmoid(_dot_nt(h, wm_ref[...]))
    y_a = _dot(ya_ref[...], wua_ref[...])

    bg = jax.nn.sigmoid(glt_ref[...])
    pieces = []
    for hd in range(NSA_HEADS):
        rows = slice(HEAD_DIM * hd, HEAD_DIM * (hd + 1))
        pieces.append(bg[3 * hd:3 * hd + 1, :] * oct_ref[rows, :]
                      + bg[3 * hd + 1:3 * hd + 2, :] * ost_ref[rows, :]
                      + bg[3 * hd + 2:3 * hd + 3, :] * owt_ref[rows, :])
    yb_t = jnp.concatenate(pieces, axis=0)
    y_b = _dot(yb_t.T.astype(BF16), wub_ref[...])

    mx = jnp.maximum(jnp.maximum(l0, l1), l2)
    e0, e1, e2 = jnp.exp(l0 - mx), jnp.exp(l1 - mx), jnp.exp(l2 - mx)
    inv = 1.0 / (e0 + e1 + e2)
    yc = (e0 * inv) * o0 + (e1 * inv) * o1 + (e2 * inv) * o2
    y_c = _dot(yc.astype(BF16), wuc_ref[...])

    d = x.shape[1]
    merged = gates[:, 0:d] * y_a + gates[:, d:2 * d] * y_b + gates[:, 2 * d:3 * d] * y_c
    out_ref[...] = x + _dot(merged.astype(BF16), wo_ref[...])


def _merge(x, g, ya, oct_, ost, owt, glt, dil, w_in, w_up_a, w_up_b, w_up_c, w_out):
    s, d = x.shape
    tm = min(MERGE_ROWS, s)
    w_m = w_in[IN_OFFSETS[12]:, :].astype(BF16)
    wuc = w_up_c.reshape(DIL_HEADS_PER_GROUP, HEAD_DIM, d)
    wuc = jnp.pad(wuc, ((0, 0), (0, LANES - HEAD_DIM), (0, 0))).reshape(DIL_BLOCK_W, d).astype(BF16)
    weights = [w_m, w_up_a.astype(BF16), w_up_b.astype(BF16), wuc, w_out.astype(BF16)]

    def rows(n):
        return pl.BlockSpec((tm, n), lambda i: (i, 0))

    def cols(n):
        return pl.BlockSpec((n, tm), lambda i: (0, i))

    dil_args, dil_specs = [], []
    for (o, lse), (_, dl) in zip(dil, DIL_GROUPS):
        dil_args += [o, lse]
        dil_specs += [pl.BlockSpec((tm // dl, dl * DIL_BLOCK_W), lambda i: (i, 0))] * 2
    return pl.pallas_call(
        _merge_body,
        out_shape=jax.ShapeDtypeStruct((s, d), F32),
        grid=(s // tm,),
        in_specs=[rows(d), _resident((1, d)), rows(LRU_WIDTH), cols(NSA_Q_W), cols(NSA_Q_W),
                  cols(NSA_Q_W), cols(GATE_ROWS)] + dil_specs + [_resident(w.shape) for w in weights],
        out_specs=rows(d),
        scratch_shapes=[pltpu.VMEM((2 * len(DIL_GROUPS), DIL_BLOCK_W // LANES, tm, LANES), F32)],
        compiler_params=_cparams(("parallel",)),
        name="merge",
    )(x, g.reshape(1, d), ya, oct_, ost, owt, glt, *dil_args, *weights)


def _mixer(x, mix_norm, w_in, conv_w, conv_b, lru_wa, lru_ba, lru_wi, lru_bi, lru_lambda,
           cmp_pos_k, cmp_pos_v, cmp_k_w1, cmp_k_w2, cmp_v_w1, cmp_v_w2,
           w_up_a, w_up_b, w_up_c, w_out):
    a2, qt, kvc, ksw, vswt, glt, *c = _proj(x, mix_norm, w_in)
    ya = _lru(a2, conv_w, conv_b, lru_wa, lru_ba, lru_wi, lru_bi, lru_lambda)
    kc = _compress(kvc[:, :NSA_KV_W], cmp_pos_k, cmp_k_w1, cmp_k_w2, transposed=False)
    vct = _compress(kvc[:, NSA_KV_W:], cmp_pos_v, cmp_v_w1, cmp_v_w2, transposed=True)
    oct_, sel = _cmp_topk(qt, kc, vct)
    ost = _band(qt, ksw, vswt, sel, mode="sel")
    owt = _band(qt, ksw, vswt, None, mode="win")
    dil = [_dilated_group(c[gi], gi) for gi in range(len(DIL_GROUPS))]
    return _merge(x, mix_norm, ya, oct_, ost, owt, glt, dil, w_in, w_up_a, w_up_b, w_up_c, w_out)


def kernel(x, ffn1_norm, ffn1_w1, ffn1_w3, ffn1_w2, mix_norm, w_in, conv_w, conv_b, lru_wa, lru_ba, lru_wi, lru_bi, lru_lambda, cmp_pos_k, cmp_pos_v, cmp_k_w1, cmp_k_w2, cmp_v_w1, cmp_v_w2, w_up_a, w_up_b, w_up_c, w_out, ffn2_norm, ffn2_w1, ffn2_w3, ffn2_w2, final_norm):
    batch, s, d = x.shape
    assert batch == 1 and d == D_MODEL
    depth = w_in.shape[0]
    w_in_t = jnp.swapaxes(w_in, 1, 2)
    y = x.reshape(s, d)
    for l in range(depth):
        y = _ffn(y, ffn1_norm[l], ffn1_w1[l], ffn1_w3[l], ffn1_w2[l], final_norm, final=False)
        y = _mixer(y, mix_norm[l], w_in_t[l], conv_w[l], conv_b[l], lru_wa[l], lru_ba[l], lru_wi[l],
                   lru_bi[l], lru_lambda[l], cmp_pos_k[l], cmp_pos_v[l], cmp_k_w1[l], cmp_k_w2[l],
                   cmp_v_w1[l], cmp_v_w2[l], w_up_a[l], w_up_b[l], w_up_c[l], w_out[l])
        y = _ffn(y, ffn2_norm[l], ffn2_w1[l], ffn2_w3[l], ffn2_w2[l], final_norm,
                 final=(l == depth - 1))
    return y.reshape(batch, s, d)
```

```python
import functools

import jax
import jax.numpy as jnp
import numpy as np
from jax import lax
from jax.experimental import pallas as pl
from jax.experimental.pallas import tpu as pltpu

F32 = jnp.float32
BF16 = jnp.bfloat16

D_MODEL = 1024
D_FF = 2816
HEAD_DIM = 64
RMS_EPS = 1e-6
NEG_INF = -1e30
SCALE = HEAD_DIM ** -0.5
QK_SCALE_LOG2 = float(SCALE * np.log2(np.e))

LRU_WIDTH = 768
CONV_WIDTH = 4
LRU_C = 8.0

NSA_HEADS = 12
NSA_KV_HEADS = 3
NSA_GROUP = 4
NSA_Q_W = 768
NSA_KV_W = 192
CMP_BLOCK = 32
CMP_STRIDE = 16
SEL_BLOCK = 64
SEL_TOPN = 16
WIN = 512
FORCE_SCORE = 1e9

DIL_GROUPS = ((128, 1), (512, 4), (2048, 16))
DIL_HEADS_PER_GROUP = 4
DIL_W = 768

IN_SPLITS = (768, 768, 768, 192, 192, 192, 192, 192, 192, 36, 768, 768, 768, 1024, 1024, 1024)
IN_OFFSETS = tuple(int(o) for o in np.cumsum(IN_SPLITS)[:-1])

LANES = 128
V7X_VMEM_LIMIT_BYTES = 56 * 1024 * 1024

FF_CHUNK = 256
FFN_ROWS = 1024
PROJ_ROWS = 512
LRU_ROWS = 256
MERGE_ROWS = 256
Q_BLOCK = 128
KEY_CHUNK = 256
BAND_Q = 256
SUM_ROWS = 16
BAND_UNROLL = 4
DIL_SUBS = 4
GATE_ROWS = 40
CMP_TILE = 64
DIL_BLOCK_W = DIL_HEADS_PER_GROUP * LANES
DIL_GROUP_W = 3 * DIL_BLOCK_W


def _cparams(sem):
    return pltpu.CompilerParams(dimension_semantics=sem, vmem_limit_bytes=V7X_VMEM_LIMIT_BYTES)


def _resident(shape):
    nd = len(shape)
    return pl.BlockSpec(shape, lambda *_: (0,) * nd, pipeline_mode=pl.Buffered(1))


def _rms(x, g):
    return x * lax.rsqrt(jnp.mean(x * x, axis=-1, keepdims=True) + RMS_EPS) * g


def _gelu_tanh(x):
    c = np.float32(np.sqrt(2.0 / np.pi))
    return x * (0.5 * (1.0 + jnp.tanh(c * (x + 0.044715 * (x * x * x)))))


def _dot(a, b):
    return jnp.dot(a, b, preferred_element_type=F32)


def _dot_nt(a, b):
    return lax.dot_general(a, b, (((1,), (1,)), ((), ())), preferred_element_type=F32)


def _ffn_body(x_ref, g_ref, w1_ref, w3_ref, w2_ref, fg_ref, o_ref, h_ref, acc_ref, *, final):
    x = x_ref[...]
    h_ref[...] = _rms(x, g_ref[...]).astype(BF16)
    acc_ref[...] = jnp.zeros_like(acc_ref)

    def chunk(c, carry):
        h = h_ref[...]
        a = _dot(h, w1_ref[c])
        b = _dot(h, w3_ref[c])
        gated = (a * jax.nn.sigmoid(a) * b).astype(BF16)
        acc_ref[...] += _dot(gated, w2_ref[c])
        return carry

    lax.fori_loop(0, w1_ref.shape[0], chunk, 0)
    y = x + 0.5 * acc_ref[...]
    if final:
        y = _rms(y, fg_ref[...])
    o_ref[...] = y


def _ffn(x, g, w1, w3, w2, fg, *, final):
    s, d = x.shape
    nc = D_FF // FF_CHUNK
    w1c = w1.astype(BF16).reshape(d, nc, FF_CHUNK).transpose(1, 0, 2)
    w3c = w3.astype(BF16).reshape(d, nc, FF_CHUNK).transpose(1, 0, 2)
    w2c = w2.astype(BF16).reshape(nc, FF_CHUNK, d)
    tm = min(FFN_ROWS, s)
    row = pl.BlockSpec((tm, d), lambda i: (i, 0))
    return pl.pallas_call(
        functools.partial(_ffn_body, final=final),
        out_shape=jax.ShapeDtypeStruct((s, d), F32),
        grid=(s // tm,),
        in_specs=[row, _resident((1, d)), _resident(w1c.shape), _resident(w3c.shape),
                  _resident(w2c.shape), _resident((1, d))],
        out_specs=row,
        scratch_shapes=[pltpu.VMEM((tm, d), BF16), pltpu.VMEM((tm, d), F32)],
        compiler_params=_cparams(("parallel",)),
        name="ffn_final" if final else "ffn",
    )(x, g.reshape(1, d), w1c, w3c, w2c, fg.reshape(1, d))


def _proj_body(x_ref, g_ref, w_a_ref, w_q_ref, w_kvc_ref, w_ksw_ref, w_vsw_ref, w_gl_ref,
               w_c0_ref, w_c1_ref, w_c2_ref,
               a2_ref, qt_ref, kvc_ref, ksw_ref, vswt_ref, glt_ref, c0_ref, c1_ref, c2_ref, slab_s):
    tm = x_ref.shape[0]
    h = _rms(x_ref[...], g_ref[...]).astype(BF16)
    a2_ref[...] = _dot_nt(h, w_a_ref[...])
    qt_ref[...] = (_dot_nt(w_q_ref[...], h) * QK_SCALE_LOG2).astype(BF16)
    kvc_ref[...] = _dot_nt(h, w_kvc_ref[...])
    ksw_ref[...] = _dot_nt(h, w_ksw_ref[...]).astype(BF16)
    for j in range(tm // KEY_CHUNK):
        vswt_ref[j] = _dot_nt(w_vsw_ref[...], h[KEY_CHUNK * j:KEY_CHUNK * (j + 1), :]).astype(BF16)
    glt_ref[...] = _dot_nt(w_gl_ref[...], h)
    for w_ref, c_ref, (_, dil) in zip((w_c0_ref, w_c1_ref, w_c2_ref), (c0_ref, c1_ref, c2_ref), DIL_GROUPS):
        for part, scale in ((slice(0, DIL_BLOCK_W), SCALE), (slice(DIL_BLOCK_W, DIL_GROUP_W), 1.0)):
            res = _dot_nt(h, w_ref[part, :])
            if scale != 1.0:
                res = res * scale
            if dil == 1:
                c_ref[:, part] = res.astype(BF16)
                continue
            for sl in range(part.start // LANES, part.stop // LANES):
                slab_s[sl] = res[:, LANES * sl - part.start:LANES * (sl + 1) - part.start]
                for r in range(dil):
                    col = r * DIL_GROUP_W + LANES * sl
                    c_ref[:, col:col + LANES] = slab_s[sl, pl.ds(r, tm // dil, stride=dil), :].astype(BF16)


def _pad_heads(w, n_heads):
    k = w.shape[1]
    w = w.reshape(n_heads, HEAD_DIM, k)
    w = jnp.pad(w, ((0, 0), (0, LANES - HEAD_DIM), (0, 0)))
    return w.reshape(n_heads * LANES, k)


def _proj(x, g, w_in_t):
    s, d = x.shape
    (a_x, a_gate, b_q, b_kc, b_vc, b_ks, b_vs, b_kw, b_vw, b_gate,
     c_q, c_k, c_v, _, _, _) = jnp.split(w_in_t, list(IN_OFFSETS), axis=0)
    w_a = jnp.concatenate([a_x, a_gate], axis=0)
    w_kvc = jnp.concatenate([b_kc, b_vc], axis=0)
    w_ksw = jnp.concatenate([_pad_heads(b_ks, 3), _pad_heads(b_kw, 3)], axis=0)
    w_vsw = jnp.concatenate([b_vs, b_vw], axis=0)
    w_gl = jnp.pad(b_gate, ((0, GATE_ROWS - b_gate.shape[0]), (0, 0)))
    gw = DIL_HEADS_PER_GROUP * HEAD_DIM
    w_c = [jnp.concatenate([_pad_heads(t[gi * gw:(gi + 1) * gw], DIL_HEADS_PER_GROUP)
                            for t in (c_q, c_k, c_v)], axis=0) for gi in range(len(DIL_GROUPS))]
    weights = [w.astype(BF16) for w in [w_a, b_q, w_kvc, w_ksw, w_vsw, w_gl] + w_c]
    tm = PROJ_ROWS
    assert tm % KEY_CHUNK == 0 and s % tm == 0

    def rows(n):
        return pl.BlockSpec((tm, n), lambda i: (i, 0))

    def cols(n):
        return pl.BlockSpec((n, tm), lambda i: (0, i))

    out_shape = [
        jax.ShapeDtypeStruct((s, 2 * LRU_WIDTH), F32),
        jax.ShapeDtypeStruct((NSA_Q_W, s), BF16),
        jax.ShapeDtypeStruct((s, 2 * NSA_KV_W), F32),
        jax.ShapeDtypeStruct((s, 6 * LANES), BF16),
        jax.ShapeDtypeStruct((s // KEY_CHUNK, 2 * NSA_KV_W, KEY_CHUNK), BF16),
        jax.ShapeDtypeStruct((GATE_ROWS, s), F32),
    ]
    out_specs = [rows(2 * LRU_WIDTH), cols(NSA_Q_W), rows(2 * NSA_KV_W), rows(6 * LANES),
                 pl.BlockSpec((tm // KEY_CHUNK, 2 * NSA_KV_W, KEY_CHUNK), lambda i: (i, 0, 0)), cols(GATE_ROWS)]
    for _, dil in DIL_GROUPS:
        out_shape.append(jax.ShapeDtypeStruct((s // dil, dil * DIL_GROUP_W), BF16))
        out_specs.append(pl.BlockSpec((tm // dil, dil * DIL_GROUP_W), lambda i: (i, 0)))
    return pl.pallas_call(
        _proj_body,
        out_shape=out_shape,
        grid=(s // tm,),
        in_specs=[rows(d), _resident((1, d))] + [_resident(w.shape) for w in weights],
        out_specs=out_specs,
        scratch_shapes=[pltpu.VMEM((DIL_GROUP_W // LANES, tm, LANES), F32)],
        compiler_params=_cparams(("parallel",)),
        name="proj",
    )(x, g.reshape(1, d), *weights)


def _lru_body(a2_ref, cw_ref, cb_ref, wa_ref, ba_ref, wi_ref, bi_ref, lam_ref, y_ref,
              xbuf, a_s, u_s, h_s, hc):
    t_rows = y_ref.shape[0]
    w = LRU_WIDTH

    @pl.when(pl.program_id(0) == 0)
    def _():
        xbuf[0:8, :] = jnp.zeros((8, w), F32)
        hc[...] = jnp.zeros_like(hc)

    x = a2_ref[:, 0:w]
    xbuf[8:8 + t_rows, :] = x
    cw = cw_ref[...]
    xc = cb_ref[...] + cw[0:1] * xbuf[5:5 + t_rows, :]
    xc = xc + cw[1:2] * xbuf[6:6 + t_rows, :]
    xc = xc + cw[2:3] * xbuf[7:7 + t_rows, :]
    xc = xc + cw[3:4] * x
    xbuf[0:8, :] = x[t_rows - 8:t_rows, :]

    xcb = xc.astype(BF16)
    ra, ri = [], []
    for p in range(wa_ref.shape[0]):
        blk = xcb[:, 256 * p:256 * (p + 1)]
        ra.append(_dot(blk, wa_ref[p]))
        ri.append(_dot(blk, wi_ref[p]))
    r = jax.nn.sigmoid(jnp.concatenate(ra, axis=1) + ba_ref[...])
    ig = jax.nn.sigmoid(jnp.concatenate(ri, axis=1) + bi_ref[...])
    z = -lam_ref[...]
    softplus = jnp.maximum(z, 0.0) + jnp.log1p(jnp.exp(-jnp.abs(z)))
    log_a = (-LRU_C * r) * softplus
    a_s[...] = jnp.exp(log_a)
    u_s[...] = jnp.sqrt(1.0 - jnp.exp(2.0 * log_a)) * (ig * xc)

    row = lax.broadcasted_iota(jnp.int32, (8, w), 0)

    def group(gidx, hprev):
        base = pl.multiple_of(gidx * 8, 8)
        a = a_s[pl.ds(base, 8), :]
        b = u_s[pl.ds(base, 8), :]
        for dshift in (1, 2, 4):
            ok = row >= dshift
            a_sh = pltpu.roll(a, dshift, 0)
            b_sh = pltpu.roll(b, dshift, 0)
            b = jnp.where(ok, a * b_sh + b, b)
            a = jnp.where(ok, a * a_sh, a)
        h8 = a * hprev + b
        h_s[pl.ds(base, 8), :] = h8
        return jnp.broadcast_to(h8[7:8, :], (8, w))

    hc[...] = lax.fori_loop(0, t_rows // 8, group, hc[...])
    y_ref[...] = (h_s[...] * _gelu_tanh(a2_ref[:, w:2 * w])).astype(BF16)


def _pair_block_diag(wb):
    z = jnp.zeros((128, 128), wb.dtype)
    return jnp.stack([jnp.block([[wb[2 * p], z], [z, wb[2 * p + 1]]]) for p in range(3)])


def _lru(a2, conv_w, conv_b, wa, ba, wi, bi, lam):
    s = a2.shape[0]
    w = LRU_WIDTH
    t = min(LRU_ROWS, s)
    wa2 = _pair_block_diag(wa.astype(BF16))
    wi2 = _pair_block_diag(wi.astype(BF16))
    vec = _resident((1, w))
    return pl.pallas_call(
        _lru_body,
        out_shape=jax.ShapeDtypeStruct((s, w), BF16),
        grid=(s // t,),
        in_specs=[pl.BlockSpec((t, 2 * w), lambda i: (i, 0)), _resident((CONV_WIDTH, w)), vec,
                  _resident(wa2.shape), vec, _resident(wi2.shape), vec, vec],
        out_specs=pl.BlockSpec((t, w), lambda i: (i, 0)),
        scratch_shapes=[pltpu.VMEM((t + 8, w), F32), pltpu.VMEM((t, w), F32), pltpu.VMEM((t, w), F32),
                        pltpu.VMEM((t, w), F32), pltpu.VMEM((8, w), F32)],
        compiler_params=_cparams(("arbitrary",)),
        name="lru",
    )(a2, conv_w, conv_b.reshape(1, w), wa2, ba.reshape(1, w), wi2, bi.reshape(1, w), lam.reshape(1, w))


CMP_HALF = CMP_STRIDE * NSA_KV_W


def _compress_body(x_ref, pos_ref, w1_ref, w2_ref, o_ref, *, transposed):
    hids = []
    for j in range(4):
        xa = (x_ref[:, CMP_HALF * j:CMP_HALF * (j + 1)] + pos_ref[:, :CMP_HALF]).astype(BF16)
        xb = (x_ref[:, CMP_HALF * (j + 1):CMP_HALF * (j + 2)] + pos_ref[:, CMP_HALF:]).astype(BF16)
        pre = _dot(xa, w1_ref[:CMP_HALF, :]) + _dot(xb, w1_ref[CMP_HALF:, :])
        hids.append(_gelu_tanh(pre).astype(BF16))
    hid = jnp.concatenate(hids, axis=0)
    if transposed:
        o_ref[...] = _dot_nt(w2_ref[...], hid).astype(BF16)
    else:
        o_ref[...] = _dot(hid, w2_ref[...]).astype(BF16)


def _compress(kv, pos, w1, w2, *, transposed):
    s = kv.shape[0]
    n_sel = s // SEL_BLOCK
    assert n_sel % CMP_TILE == 0
    rows = kv.reshape(n_sel, SEL_BLOCK * NSA_KV_W)
    nxt = jnp.concatenate([rows[1:, :CMP_HALF], jnp.zeros((1, CMP_HALF), kv.dtype)], axis=0)
    ext = jnp.concatenate([rows, nxt], axis=1)
    eye = jnp.eye(NSA_KV_HEADS, dtype=F32)
    w1e = jnp.einsum("ldn,hg->lhdgn", w1.reshape(CMP_BLOCK, HEAD_DIM, -1), eye)
    w1e = w1e.reshape(CMP_BLOCK * NSA_KV_W, -1).astype(BF16)
    hid = w1.shape[1]
    if transposed:
        w2e = jnp.einsum("nd,hg->gdhn", w2, eye).reshape(NSA_KV_W, NSA_KV_HEADS * hid).astype(BF16)
        out_shape = jax.ShapeDtypeStruct((NSA_KV_W, 4 * n_sel), BF16)
        out_spec = pl.BlockSpec((NSA_KV_W, 4 * CMP_TILE), lambda tt: (0, tt))
    else:
        w2p = jnp.pad(w2, ((0, 0), (0, LANES - HEAD_DIM)))
        w2e = jnp.einsum("nd,hg->hngd", w2p, eye).reshape(NSA_KV_HEADS * hid, NSA_KV_HEADS * LANES).astype(BF16)
        out_shape = jax.ShapeDtypeStruct((4 * n_sel, NSA_KV_HEADS * LANES), BF16)
        out_spec = pl.BlockSpec((4 * CMP_TILE, NSA_KV_HEADS * LANES), lambda tt: (tt, 0))
    pose = jnp.broadcast_to(pos[:, None, :], (CMP_BLOCK, NSA_KV_HEADS, HEAD_DIM)).reshape(1, -1)
    return pl.pallas_call(
        functools.partial(_compress_body, transposed=transposed),
        out_shape=out_shape,
        grid=(n_sel // CMP_TILE,),
        in_specs=[pl.BlockSpec((CMP_TILE, ext.shape[1]), lambda tt: (tt, 0)),
                  _resident(pose.shape), _resident(w1e.shape), _resident(w2e.shape)],
        out_specs=out_spec,
        compiler_params=_cparams(("parallel",)),
        name="compress_v" if transposed else "compress_k",
    )(ext, pose, w1e, w2e)


def _cmp_topk_tiles(qt_ref, kc_ref, vct_ref, oct_ref, bias_ref, *, n_tiles, n_sel, n_top):
    i = pl.program_id(0)
    n_rows = 4 * CMP_TILE * n_tiles
    n_blk = CMP_TILE * n_tiles
    lane_t = i * Q_BLOCK + lax.broadcasted_iota(jnp.int32, (1, Q_BLOCK), 1)
    r = lax.broadcasted_iota(jnp.int32, (n_rows, Q_BLOCK), 0)
    n_of_row = CMP_TILE * (r // (4 * CMP_TILE)) + r % CMP_TILE
    j_of_row = (r % (4 * CMP_TILE)) // CMP_TILE
    valid = SEL_BLOCK * n_of_row + CMP_STRIDE * j_of_row + (CMP_BLOCK - 1) <= lane_t
    has_valid = lane_t >= CMP_BLOCK - 1
    blk = lax.broadcasted_iota(jnp.int32, (n_blk, Q_BLOCK), 0)
    blk_f = blk.astype(F32)
    cur = lane_t // SEL_BLOCK
    forced = (blk == cur) | (blk == 0)
    future = blk > cur

    scores = []
    for k in range(NSA_KV_HEADS):
        kc = kc_ref[0:n_rows, LANES * k:LANES * k + HEAD_DIM]
        vct = jnp.concatenate([vct_ref[HEAD_DIM * k:HEAD_DIM * (k + 1), 0:n_rows],
                               jnp.ones((SUM_ROWS, n_rows), BF16)], axis=0)
        imp = jnp.zeros((n_blk, Q_BLOCK), F32)
        raw = [_dot(kc, qt_ref[HEAD_DIM * (NSA_GROUP * k + g):HEAD_DIM * (NSA_GROUP * k + g + 1), :])
               for g in range(NSA_GROUP)]
        for g in range(NSA_GROUP):
            h = NSA_GROUP * k + g
            s = jnp.where(valid, raw[g], NEG_INF)
            m = jnp.max(s, axis=0, keepdims=True)
            e = jnp.exp2(s - m)
            pv = _dot(vct, e.astype(BF16))
            den = jnp.maximum(pv[HEAD_DIM:HEAD_DIM + 1, :], 1e-30)
            inv = jnp.where(has_valid, 1.0 / den, 0.0)
            oct_ref[HEAD_DIM * h:HEAD_DIM * (h + 1), :] = pv[0:HEAD_DIM, :] * inv
            parts = []
            for tt in range(n_tiles):
                base = 4 * CMP_TILE * tt
                acc = e[base:base + CMP_TILE, :]
                for j in range(1, 4):
                    acc = acc + e[base + j * CMP_TILE:base + (j + 1) * CMP_TILE, :]
                parts.append(acc)
            imp = imp + jnp.concatenate(parts, axis=0) * inv
        scores.append(jnp.where(forced, -jnp.inf, jnp.where(future, -1.0, imp)))

    def pick_one(_, carry):
        out = []
        for v in carry:
            top = jnp.max(v, axis=0, keepdims=True)
            first = jnp.min(jnp.where(v == top, blk_f, float(n_blk)), axis=0, keepdims=True)
            out.append(jnp.where(blk_f == first, -jnp.inf, v))
        return tuple(out)

    assert n_top >= 2 and FORCE_SCORE > NSA_GROUP
    res = lax.fori_loop(0, n_top - 2, pick_one, tuple(scores))
    for k in range(NSA_KV_HEADS):
        bias_ref[k, 0:n_blk, :] = jnp.where(res[k] == -jnp.inf, 0.0, NEG_INF)
        if n_blk < n_sel:
            bias_ref[k, n_blk:n_sel, :] = jnp.full((n_sel - n_blk, Q_BLOCK), NEG_INF, F32)


def _cmp_topk_body(qt_ref, kc_ref, vct_ref, oct_ref, bias_ref, *, n_sel, n_top):
    total = n_sel // CMP_TILE
    need = jnp.minimum((2 * pl.program_id(0) + 1) // CMP_TILE + 1, total)
    for n_tiles in range(1, total + 1):
        pl.when(need == n_tiles)(functools.partial(
            _cmp_topk_tiles, qt_ref, kc_ref, vct_ref, oct_ref, bias_ref,
            n_tiles=n_tiles, n_sel=n_sel, n_top=n_top))


def _cmp_topk(qt, kc, vct):
    s = qt.shape[1]
    n_sel = s // SEL_BLOCK
    n_top = min(SEL_TOPN, n_sel)
    qspec = pl.BlockSpec((NSA_Q_W, Q_BLOCK), lambda i: (0, i))
    return pl.pallas_call(
        functools.partial(_cmp_topk_body, n_sel=n_sel, n_top=n_top),
        out_shape=[jax.ShapeDtypeStruct((NSA_Q_W, s), F32),
                   jax.ShapeDtypeStruct((NSA_KV_HEADS, n_sel, s), F32)],
        grid=(s // Q_BLOCK,),
        in_specs=[qspec, _resident(kc.shape), _resident(vct.shape)],
        out_specs=[qspec, pl.BlockSpec((NSA_KV_HEADS, n_sel, Q_BLOCK), lambda i: (0, 0, i))],
        compiler_params=_cparams(("parallel",)),
        name="cmp_topk",
    )(qt, kc, vct)


def _band_body(*refs, mode, k_off, v_off):
    if mode == "sel":
        qt_ref, k_ref, vt_ref, bias_ref, o_ref, acc_s, ml_s, s_s = refs
    else:
        qt_ref, k_ref, vt_ref, o_ref, acc_s, ml_s, s_s = refs
        bias_ref = None
    i = pl.program_id(0)
    wide = NSA_GROUP * BAND_Q
    blocks = KEY_CHUNK // SEL_BLOCK
    lane = lax.broadcasted_iota(jnp.int32, (1, wide), 1)
    t = i * BAND_Q + lane % BAND_Q
    krow = lax.broadcasted_iota(jnp.int32, (KEY_CHUNK, wide), 0)
    c_hi = (i * BAND_Q + (BAND_Q - 1)) // KEY_CHUNK + 1

    acc_s[...] = jnp.zeros_like(acc_s)
    ml_s[...] = jnp.full(ml_s.shape, NEG_INF, F32)
    ones = jnp.ones((SUM_ROWS, KEY_CHUNK), BF16)

    def scores(c, k):
        base = pl.multiple_of(c * KEY_CHUNK, KEY_CHUNK)
        q4 = jnp.concatenate(
            [qt_ref[HEAD_DIM * (NSA_GROUP * k + g):HEAD_DIM * (NSA_GROUP * k + g + 1), :]
             for g in range(NSA_GROUP)], axis=1)
        keys = k_ref[pl.ds(base, KEY_CHUNK), k_off + LANES * k:k_off + LANES * k + HEAD_DIM]
        s_s[k] = _dot(keys, q4)

    def softmax(c, k, masked):
        base = pl.multiple_of(c * KEY_CHUNK, KEY_CHUNK)
        s = s_s[k]
        if mode == "sel":
            rows = []
            for rblk in range(blocks):
                b = bias_ref[k, pl.ds(c * blocks + rblk, 1), :]
                b = jnp.concatenate([b] * NSA_GROUP, axis=1)
                rows.append(jnp.broadcast_to(b, (SEL_BLOCK, wide)))
            s = s + jnp.concatenate(rows, axis=0)
            if masked:
                s = jnp.where(base + krow <= t, s, NEG_INF)
        else:
            dist = t - (base + krow)
            s = jnp.where((dist >= 0) & (dist < WIN), s, NEG_INF)
        m_old = ml_s[k:k + 1, :]
        m_new = jnp.maximum(m_old, jnp.max(s, axis=0, keepdims=True))
        ml_s[k:k + 1, :] = m_new
        return jnp.exp2(m_old - m_new), jnp.exp2(s - m_new).astype(BF16)

    def accumulate(c, k, alpha, e):
        vt = jnp.concatenate([vt_ref[c, v_off + HEAD_DIM * k:v_off + HEAD_DIM * (k + 1), :], ones], axis=0)
        acc_s[k] = alpha * acc_s[k] + _dot(vt, e)

    def step(c, masked, prefetch):
        for k in range(NSA_KV_HEADS):
            alpha, e = softmax(c, k, masked)
            accumulate(c, k, alpha, e)
            if prefetch:
                scores(c + 1, k)

    if mode == "sel":
        c_lo = 0
    else:
        c_lo = jnp.maximum(i * BAND_Q - (WIN - 1), 0) // KEY_CHUNK
    for k in range(NSA_KV_HEADS):
        scores(c_lo, k)
    early_masked = mode == "win"
    n_early = c_hi - 1 - c_lo

    def group(p, carry):
        for j in range(BAND_UNROLL):
            step(c_lo + BAND_UNROLL * p + j, early_masked, True)
        return carry

    n_groups = n_early // BAND_UNROLL
    lax.fori_loop(0, n_groups, group, 0)
    rest_lo = c_lo + BAND_UNROLL * n_groups
    for j in range(BAND_UNROLL - 1):
        pl.when(rest_lo + j < c_hi - 1)(functools.partial(step, rest_lo + j, early_masked, True))
    step(c_hi - 1, True, False)

    for k in range(NSA_KV_HEADS):
        den = jnp.maximum(acc_s[k, HEAD_DIM:HEAD_DIM + 1, :], 1e-30)
        out = acc_s[k, 0:HEAD_DIM, :] * (1.0 / den)
        for g in range(NSA_GROUP):
            h = NSA_GROUP * k + g
            o_ref[HEAD_DIM * h:HEAD_DIM * (h + 1), :] = out[:, BAND_Q * g:BAND_Q * (g + 1)]


def _band(qt, ksw, vswt, sel, *, mode):
    s = qt.shape[1]
    assert (BAND_Q % KEY_CHUNK == 0 or KEY_CHUNK % BAND_Q == 0) and s % BAND_Q == 0
    qspec = pl.BlockSpec((NSA_Q_W, BAND_Q), lambda i: (0, i))
    k_off = 0 if mode == "sel" else NSA_KV_HEADS * LANES
    v_off = 0 if mode == "sel" else NSA_KV_W
    in_specs = [qspec, _resident(ksw.shape), _resident(vswt.shape)]
    args = [qt, ksw, vswt]
    if mode == "sel":
        in_specs.append(pl.BlockSpec((NSA_KV_HEADS, sel.shape[1], BAND_Q), lambda i: (0, 0, i)))
        args.append(sel)
    wide = NSA_GROUP * BAND_Q
    return pl.pallas_call(
        functools.partial(_band_body, mode=mode, k_off=k_off, v_off=v_off),
        out_shape=jax.ShapeDtypeStruct((NSA_Q_W, s), F32),
        grid=(s // BAND_Q,),
        in_specs=in_specs,
        out_specs=qspec,
        scratch_shapes=[pltpu.VMEM((NSA_KV_HEADS, HEAD_DIM + SUM_ROWS, wide), F32), pltpu.VMEM((8, wide), F32),
                        pltpu.VMEM((NSA_KV_HEADS, KEY_CHUNK, wide), F32)],
        compiler_params=_cparams(("parallel",)),
        name="nsa_" + mode,
    )(*args)


def _dil_body(q_ref, kp_ref, kc_ref, vp_ref, vc_ref, o_ref, lse_ref, *, span, subs):
    i = pl.program_id(1)
    rq = lax.broadcasted_iota(jnp.int32, (Q_BLOCK, 2 * Q_BLOCK), 0)
    ck = lax.broadcasted_iota(jnp.int32, (Q_BLOCK, 2 * Q_BLOCK), 1)
    delta = Q_BLOCK + rq - ck
    in_span = (delta >= 0) & (delta <= span)
    for sub in range(subs):
        rows = slice(Q_BLOCK * sub, Q_BLOCK * (sub + 1))
        valid = in_span & ((i * subs + sub - 1) * Q_BLOCK + ck >= 0)

        def pair(prev_ref, cur_ref, lanes):
            if sub == 0:
                return jnp.concatenate([prev_ref[:, lanes], cur_ref[rows, lanes]], axis=0)
            return cur_ref[Q_BLOCK * (sub - 1):Q_BLOCK * (sub + 1), lanes]

        raw = []
        for j in range(DIL_HEADS_PER_GROUP):
            q = q_ref[rows, LANES * j:LANES * j + HEAD_DIM]
            raw.append(_dot_nt(q, pair(kp_ref, kc_ref, slice(LANES * j, LANES * j + HEAD_DIM))))
        for j in range(DIL_HEADS_PER_GROUP):
            vv = pair(vp_ref, vc_ref, slice(LANES * j, LANES * (j + 1)))
            s = jnp.where(valid, raw[j], NEG_INF)
            m = jnp.max(s, axis=1, keepdims=True)
            e = jnp.where(valid, jnp.exp(s - m), 0.0)
            den = jnp.maximum(jnp.sum(e, axis=1, keepdims=True), 1e-30)
            p = e * (1.0 / den)
            o_ref[rows, LANES * j:LANES * (j + 1)] = _dot(p.astype(BF16), vv)
            lse_ref[rows, LANES * j:LANES * (j + 1)] = jnp.broadcast_to(m + jnp.log(den), (Q_BLOCK, LANES))


def _dilated_group(c, gi):
    window, dil = DIL_GROUPS[gi]
    span = window // dil
    assert span == Q_BLOCK
    view = c
    rows = view.shape[0]
    s = rows * dil
    subs = min(DIL_SUBS, rows // Q_BLOCK)
    big = subs * Q_BLOCK
    assert rows % big == 0
    spec = lambda tensor, prev: pl.BlockSpec(
        (Q_BLOCK, DIL_BLOCK_W) if prev else (big, DIL_BLOCK_W),
        lambda r, i: (jnp.maximum(i * subs - 1, 0) if prev else i, 3 * r + tensor))
    ospec = pl.BlockSpec((big, DIL_BLOCK_W), lambda r, i: (i, r))
    o, lse = pl.pallas_call(
        functools.partial(_dil_body, span=span, subs=subs),
        out_shape=[jax.ShapeDtypeStruct((rows, dil * DIL_BLOCK_W), F32)] * 2,
        grid=(dil, rows // big),
        in_specs=[spec(0, False), spec(1, True), spec(1, False), spec(2, True), spec(2, False)],
        out_specs=[ospec, ospec],
        compiler_params=_cparams(("parallel", "parallel")),
        name="dilated_%d" % gi,
    )(view, view, view, view, view)
    return o, lse


def _merge_body(x_ref, g_ref, ya_ref, oct_ref, ost_ref, owt_ref, glt_ref,
                o0_ref, l0_ref, o1_ref, l1_ref, o2_ref, l2_ref,
                wm_ref, wua_ref, wub_ref, wuc_ref, wo_ref, out_ref, nat_s):
    tm = x_ref.shape[0]

    def natural(view_ref, dil, slot):
        if dil == 1:
            return view_ref[...]
        for r in range(dil):
            for sl in range(DIL_BLOCK_W // LANES):
                col = r * DIL_BLOCK_W + LANES * sl
                nat_s[slot, sl, pl.ds(r, tm // dil, stride=dil), :] = view_ref[:, col:col + LANES]
        return jnp.concatenate([nat_s[slot, sl] for sl in range(DIL_BLOCK_W // LANES)], axis=1)

    dils = [dil for _, dil in DIL_GROUPS]
    o0, o1, o2 = (natural(r, dl, 2 * n) for n, (r, dl) in enumerate(zip((o0_ref, o1_ref, o2_ref), dils)))
    l0, l1, l2 = (natural(r, dl, 2 * n + 1) for n, (r, dl) in enumerate(zip((l0_ref, l1_ref, l2_ref), dils)))
    x = x_ref[...]
    h = _rms(x, g_ref[...]).astype(BF16)
    gates = jax.nn.sigmoid(_dot_nt(h, wm_ref[...]))
    y_a = _dot(ya_ref[...], wua_ref[...])

    bg = jax.nn.sigmoid(glt_ref[...])
    pieces = []
    for hd in range(NSA_HEADS):
        rows = slice(HEAD_DIM * hd, HEAD_DIM * (hd + 1))
        pieces.append(bg[3 * hd:3 * hd + 1, :] * oct_ref[rows, :]
                      + bg[3 * hd + 1:3 * hd + 2, :] * ost_ref[rows, :]
                      + bg[3 * hd + 2:3 * hd + 3, :] * owt_ref[rows, :])
    yb_t = jnp.concatenate(pieces, axis=0)
    y_b = _dot(yb_t.T.astype(BF16), wub_ref[...])

    mx = jnp.maximum(jnp.maximum(l0, l1), l2)
    e0, e1, e2 = jnp.exp(l0 - mx), jnp.exp(l1 - mx), jnp.exp(l2 - mx)
    inv = 1.0 / (e0 + e1 + e2)
    yc = (e0 * inv) * o0 + (e1 * inv) * o1 + (e2 * inv) * o2
    y_c = _dot(yc.astype(BF16), wuc_ref[...])

    d = x.shape[1]
    merged = gates[:, 0:d] * y_a + gates[:, d:2 * d] * y_b + gates[:, 2 * d:3 * d] * y_c
    out_ref[...] = x + _dot(merged.astype(BF16), wo_ref[...])


def _merge(x, g, ya, oct_, ost, owt, glt, dil, w_in, w_up_a, w_up_b, w_up_c, w_out):
    s, d = x.shape
    tm = min(MERGE_ROWS, s)
    w_m = w_in[IN_OFFSETS[12]:, :].astype(BF16)
    wuc = w_up_c.reshape(DIL_HEADS_PER_GROUP, HEAD_DIM, d)
    wuc = jnp.pad(wuc, ((0, 0), (0, LANES - HEAD_DIM), (0, 0))).reshape(DIL_BLOCK_W, d).astype(BF16)
    weights = [w_m, w_up_a.astype(BF16), w_up_b.astype(BF16), wuc, w_out.astype(BF16)]

    def rows(n):
        return pl.BlockSpec((tm, n), lambda i: (i, 0))

    def cols(n):
        return pl.BlockSpec((n, tm), lambda i: (0, i))

    dil_args, dil_specs = [], []
    for (o, lse), (_, dl) in zip(dil, DIL_GROUPS):
        dil_args += [o, lse]
        dil_specs += [pl.BlockSpec((tm // dl, dl * DIL_BLOCK_W), lambda i: (i, 0))] * 2
    return pl.pallas_call(
        _merge_body,
        out_shape=jax.ShapeDtypeStruct((s, d), F32),
        grid=(s // tm,),
        in_specs=[rows(d), _resident((1, d)), rows(LRU_WIDTH), cols(NSA_Q_W), cols(NSA_Q_W),
                  cols(NSA_Q_W), cols(GATE_ROWS)] + dil_specs + [_resident(w.shape) for w in weights],
        out_specs=rows(d),
        scratch_shapes=[pltpu.VMEM((2 * len(DIL_GROUPS), DIL_BLOCK_W // LANES, tm, LANES), F32)],
        compiler_params=_cparams(("parallel",)),
        name="merge",
    )(x, g.reshape(1, d), ya, oct_, ost, owt, glt, *dil_args, *weights)


def _mixer(x, mix_norm, w_in, conv_w, conv_b, lru_wa, lru_ba, lru_wi, lru_bi, lru_lambda,
           cmp_pos_k, cmp_pos_v, cmp_k_w1, cmp_k_w2, cmp_v_w1, cmp_v_w2,
           w_up_a, w_up_b, w_up_c, w_out):
    a2, qt, kvc, ksw, vswt, glt, *c = _proj(x, mix_norm, w_in)
    ya = _lru(a2, conv_w, conv_b, lru_wa, lru_ba, lru_wi, lru_bi, lru_lambda)
    kc = _compress(kvc[:, :NSA_KV_W], cmp_pos_k, cmp_k_w1, cmp_k_w2, transposed=False)
    vct = _compress(kvc[:, NSA_KV_W:], cmp_pos_v, cmp_v_w1, cmp_v_w2, transposed=True)
    oct_, sel = _cmp_topk(qt, kc, vct)
    ost = _band(qt, ksw, vswt, sel, mode="sel")
    owt = _band(qt, ksw, vswt, None, mode="win")
    dil = [_dilated_group(c[gi], gi) for gi in range(len(DIL_GROUPS))]
    return _merge(x, mix_norm, ya, oct_, ost, owt, glt, dil, w_in, w_up_a, w_up_b, w_up_c, w_out)


def kernel(x, ffn1_norm, ffn1_w1, ffn1_w3, ffn1_w2, mix_norm, w_in, conv_w, conv_b, lru_wa, lru_ba, lru_wi, lru_bi, lru_lambda, cmp_pos_k, cmp_pos_v, cmp_k_w1, cmp_k_w2, cmp_v_w1, cmp_v_w2, w_up_a, w_up_b, w_up_c, w_out, ffn2_norm, ffn2_w1, ffn2_w3, ffn2_w2, final_norm):
    batch, s, d = x.shape
    assert batch == 1 and d == D_MODEL
    depth = w_in.shape[0]
    w_in_t = jnp.swapaxes(w_in, 1, 2)
    y = x.reshape(s, d)
    for l in range(depth):
        y = _ffn(y, ffn1_norm[l], ffn1_w1[l], ffn1_w3[l], ffn1_w2[l], final_norm, final=False)
        y = _mixer(y, mix_norm[l], w_in_t[l], conv_w[l], conv_b[l], lru_wa[l], lru_ba[l], lru_wi[l],
                   lru_bi[l], lru_lambda[l], cmp_pos_k[l], cmp_pos_v[l], cmp_k_w1[l], cmp_k_w2[l],
                   cmp_v_w1[l], cmp_v_w2[l], w_up_a[l], w_up_b[l], w_up_c[l], w_out[l])
        y = _ffn(y, ffn2_norm[l], ffn2_w1[l], ffn2_w3[l], ffn2_w2[l], final_norm,
                 final=(l == depth - 1))
    return y.reshape(batch, s, d)
```

```python
import functools

import jax
import jax.numpy as jnp
import numpy as np
from jax import lax
from jax.experimental import pallas as pl
from jax.experimental.pallas import tpu as pltpu

F32 = jnp.float32
BF16 = jnp.bfloat16

D_MODEL = 1024
D_FF = 2816
HEAD_DIM = 64
RMS_EPS = 1e-6
NEG_INF = -1e30
SCALE = HEAD_DIM ** -0.5
QK_SCALE_LOG2 = float(SCALE * np.log2(np.e))

LRU_WIDTH = 768
CONV_WIDTH = 4
LRU_C = 8.0

NSA_HEADS = 12
NSA_KV_HEADS = 3
NSA_GROUP = 4
NSA_Q_W = 768
NSA_KV_W = 192
CMP_BLOCK = 32
CMP_STRIDE = 16
SEL_BLOCK = 64
SEL_TOPN = 16
WIN = 512
FORCE_SCORE = 1e9

DIL_GROUPS = ((128, 1), (512, 4), (2048, 16))
DIL_HEADS_PER_GROUP = 4
DIL_W = 768

IN_SPLITS = (768, 768, 768, 192, 192, 192, 192, 192, 192, 36, 768, 768, 768, 1024, 1024, 1024)
IN_OFFSETS = tuple(int(o) for o in np.cumsum(IN_SPLITS)[:-1])

LANES = 128
V7X_VMEM_LIMIT_BYTES = 56 * 1024 * 1024

FF_CHUNK = 256
FFN_ROWS = 1024
PROJ_ROWS = 512
LRU_ROWS = 256
MERGE_ROWS = 256
Q_BLOCK = 128
KEY_CHUNK = 256
BAND_Q = 256
SUM_ROWS = 16
BAND_UNROLL = 8
DIL_SUBS = 8
GATE_ROWS = 40
CMP_TILE = 64
DIL_BLOCK_W = DIL_HEADS_PER_GROUP * LANES
DIL_GROUP_W = 3 * DIL_BLOCK_W


def _cparams(sem):
    return pltpu.CompilerParams(dimension_semantics=sem, vmem_limit_bytes=V7X_VMEM_LIMIT_BYTES)


def _resident(shape):
    nd = len(shape)
    return pl.BlockSpec(shape, lambda *_: (0,) * nd, pipeline_mode=pl.Buffered(1))


def _rms(x, g):
    return x * lax.rsqrt(jnp.mean(x * x, axis=-1, keepdims=True) + RMS_EPS) * g


def _gelu_tanh(x):
    c = np.float32(np.sqrt(2.0 / np.pi))
    return x * (0.5 * (1.0 + jnp.tanh(c * (x + 0.044715 * (x * x * x)))))


def _dot(a, b):
    return jnp.dot(a, b, preferred_element_type=F32)


def _dot_nt(a, b):
    return lax.dot_general(a, b, (((1,), (1,)), ((), ())), preferred_element_type=F32)


def _ffn_body(x_ref, g_ref, w1_ref, w3_ref, w2_ref, fg_ref, o_ref, h_ref, acc_ref, *, final):
    x = x_ref[...]
    h_ref[...] = _rms(x, g_ref[...]).astype(BF16)
    acc_ref[...] = jnp.zeros_like(acc_ref)

    def chunk(c, carry):
        h = h_ref[...]
        a = _dot(h, w1_ref[c])
        b = _dot(h, w3_ref[c])
        gated = (a * jax.nn.sigmoid(a) * b).astype(BF16)
        acc_ref[...] += _dot(gated, w2_ref[c])
        return carry

    lax.fori_loop(0, w1_ref.shape[0], chunk, 0)
    y = x + 0.5 * acc_ref[...]
    if final:
        y = _rms(y, fg_ref[...])
    o_ref[...] = y


def _ffn(x, g, w1, w3, w2, fg, *, final):
    s, d = x.shape
    nc = D_FF // FF_CHUNK
    w1c = w1.astype(BF16).reshape(d, nc, FF_CHUNK).transpose(1, 0, 2)
    w3c = w3.astype(BF16).reshape(d, nc, FF_CHUNK).transpose(1, 0, 2)
    w2c = w2.astype(BF16).reshape(nc, FF_CHUNK, d)
    tm = min(FFN_ROWS, s)
    row = pl.BlockSpec((tm, d), lambda i: (i, 0))
    return pl.pallas_call(
        functools.partial(_ffn_body, final=final),
        out_shape=jax.ShapeDtypeStruct((s, d), F32),
        grid=(s // tm,),
        in_specs=[row, _resident((1, d)), _resident(w1c.shape), _resident(w3c.shape),
                  _resident(w2c.shape), _resident((1, d))],
        out_specs=row,
        scratch_shapes=[pltpu.VMEM((tm, d), BF16), pltpu.VMEM((tm, d), F32)],
        compiler_params=_cparams(("parallel",)),
        name="ffn_final" if final else "ffn",
    )(x, g.reshape(1, d), w1c, w3c, w2c, fg.reshape(1, d))


def _proj_body(x_ref, g_ref, w_a_ref, w_q_ref, w_kvc_ref, w_ksw_ref, w_vsw_ref, w_gl_ref,
               w_c0_ref, w_c1_ref, w_c2_ref,
               a2_ref, qt_ref, kvc_ref, ksw_ref, vswt_ref, glt_ref, c0_ref, c1_ref, c2_ref, slab_s):
    tm = x_ref.shape[0]
    h = _rms(x_ref[...], g_ref[...]).astype(BF16)
    a2_ref[...] = _dot_nt(h, w_a_ref[...])
    qt_ref[...] = (_dot_nt(w_q_ref[...], h) * QK_SCALE_LOG2).astype(BF16)
    kvc_ref[...] = _dot_nt(h, w_kvc_ref[...])
    ksw_ref[...] = _dot_nt(h, w_ksw_ref[...]).astype(BF16)
    for j in range(tm // KEY_CHUNK):
        vswt_ref[j] = _dot_nt(w_vsw_ref[...], h[KEY_CHUNK * j:KEY_CHUNK * (j + 1), :]).astype(BF16)
    glt_ref[...] = _dot_nt(w_gl_ref[...], h)
    for w_ref, c_ref, (_, dil) in zip((w_c0_ref, w_c1_ref, w_c2_ref), (c0_ref, c1_ref, c2_ref), DIL_GROUPS):
        for part, scale in ((slice(0, DIL_BLOCK_W), SCALE), (slice(DIL_BLOCK_W, DIL_GROUP_W), 1.0)):
            res = _dot_nt(h, w_ref[part, :])
            if scale != 1.0:
                res = res * scale
            if dil == 1:
                c_ref[:, part] = res.astype(BF16)
                continue
            for sl in range(part.start // LANES, part.stop // LANES):
                slab_s[sl] = res[:, LANES * sl - part.start:LANES * (sl + 1) - part.start]
                for r in range(dil):
                    col = r * DIL_GROUP_W + LANES * sl
                    c_ref[:, col:col + LANES] = slab_s[sl, pl.ds(r, tm // dil, stride=dil), :].astype(BF16)


def _pad_heads(w, n_heads):
    k = w.shape[1]
    w = w.reshape(n_heads, HEAD_DIM, k)
    w = jnp.pad(w, ((0, 0), (0, LANES - HEAD_DIM), (0, 0)))
    return w.reshape(n_heads * LANES, k)


def _proj(x, g, w_in_t):
    s, d = x.shape
    (a_x, a_gate, b_q, b_kc, b_vc, b_ks, b_vs, b_kw, b_vw, b_gate,
     c_q, c_k, c_v, _, _, _) = jnp.split(w_in_t, list(IN_OFFSETS), axis=0)
    w_a = jnp.concatenate([a_x, a_gate], axis=0)
    w_kvc = jnp.concatenate([b_kc, b_vc], axis=0)
    w_ksw = jnp.concatenate([_pad_heads(b_ks, 3), _pad_heads(b_kw, 3)], axis=0)
    w_vsw = jnp.concatenate([b_vs, b_vw], axis=0)
    w_gl = jnp.pad(b_gate, ((0, GATE_ROWS - b_gate.shape[0]), (0, 0)))
    gw = DIL_HEADS_PER_GROUP * HEAD_DIM
    w_c = [jnp.concatenate([_pad_heads(t[gi * gw:(gi + 1) * gw], DIL_HEADS_PER_GROUP)
                            for t in (c_q, c_k, c_v)], axis=0) for gi in range(len(DIL_GROUPS))]
    weights = [w.astype(BF16) for w in [w_a, b_q, w_kvc, w_ksw, w_vsw, w_gl] + w_c]
    tm = PROJ_ROWS
    assert tm % KEY_CHUNK == 0 and s % tm == 0

    def rows(n):
        return pl.BlockSpec((tm, n), lambda i: (i, 0))

    def cols(n):
        return pl.BlockSpec((n, tm), lambda i: (0, i))

    out_shape = [
        jax.ShapeDtypeStruct((s, 2 * LRU_WIDTH), F32),
        jax.ShapeDtypeStruct((NSA_Q_W, s), BF16),
        jax.ShapeDtypeStruct((s, 2 * NSA_KV_W), F32),
        jax.ShapeDtypeStruct((s, 6 * LANES), BF16),
        jax.ShapeDtypeStruct((s // KEY_CHUNK, 2 * NSA_KV_W, KEY_CHUNK), BF16),
        jax.ShapeDtypeStruct((GATE_ROWS, s), F32),
    ]
    out_specs = [rows(2 * LRU_WIDTH), cols(NSA_Q_W), rows(2 * NSA_KV_W), rows(6 * LANES),
                 pl.BlockSpec((tm // KEY_CHUNK, 2 * NSA_KV_W, KEY_CHUNK), lambda i: (i, 0, 0)), cols(GATE_ROWS)]
    for _, dil in DIL_GROUPS:
        out_shape.append(jax.ShapeDtypeStruct((s // dil, dil * DIL_GROUP_W), BF16))
        out_specs.append(pl.BlockSpec((tm // dil, dil * DIL_GROUP_W), lambda i: (i, 0)))
    return pl.pallas_call(
        _proj_body,
        out_shape=out_shape,
        grid=(s // tm,),
        in_specs=[rows(d), _resident((1, d))] + [_resident(w.shape) for w in weights],
        out_specs=out_specs,
        scratch_shapes=[pltpu.VMEM((DIL_GROUP_W // LANES, tm, LANES), F32)],
        compiler_params=_cparams(("parallel",)),
        name="proj",
    )(x, g.reshape(1, d), *weights)


def _lru_body(a2_ref, cw_ref, cb_ref, wa_ref, ba_ref, wi_ref, bi_ref, lam_ref, y_ref,
              xbuf, a_s, u_s, h_s, hc):
    t_rows = y_ref.shape[0]
    w = LRU_WIDTH

    @pl.when(pl.program_id(0) == 0)
    def _():
        xbuf[0:8, :] = jnp.zeros((8, w), F32)
        hc[...] = jnp.zeros_like(hc)

    x = a2_ref[:, 0:w]
    xbuf[8:8 + t_rows, :] = x
    cw = cw_ref[...]
    xc = cb_ref[...] + cw[0:1] * xbuf[5:5 + t_rows, :]
    xc = xc + cw[1:2] * xbuf[6:6 + t_rows, :]
    xc = xc + cw[2:3] * xbuf[7:7 + t_rows, :]
    xc = xc + cw[3:4] * x
    xbuf[0:8, :] = x[t_rows - 8:t_rows, :]

    xcb = xc.astype(BF16)
    ra, ri = [], []
    for p in range(wa_ref.shape[0]):
        blk = xcb[:, 256 * p:256 * (p + 1)]
        ra.append(_dot(blk, wa_ref[p]))
        ri.append(_dot(blk, wi_ref[p]))
    r = jax.nn.sigmoid(jnp.concatenate(ra, axis=1) + ba_ref[...])
    ig = jax.nn.sigmoid(jnp.concatenate(ri, axis=1) + bi_ref[...])
    z = -lam_ref[...]
    softplus = jnp.maximum(z, 0.0) + jnp.log1p(jnp.exp(-jnp.abs(z)))
    log_a = (-LRU_C * r) * softplus
    a_s[...] = jnp.exp(log_a)
    u_s[...] = jnp.sqrt(1.0 - jnp.exp(2.0 * log_a)) * (ig * xc)

    row = lax.broadcasted_iota(jnp.int32, (8, w), 0)

    def group(gidx, hprev):
        base = pl.multiple_of(gidx * 8, 8)
        a = a_s[pl.ds(base, 8), :]
        b = u_s[pl.ds(base, 8), :]
        for dshift in (1, 2, 4):
            ok = row >= dshift
            a_sh = pltpu.roll(a, dshift, 0)
            b_sh = pltpu.roll(b, dshift, 0)
            b = jnp.where(ok, a * b_sh + b, b)
            a = jnp.where(ok, a * a_sh, a)
        h8 = a * hprev + b
        h_s[pl.ds(base, 8), :] = h8
        return jnp.broadcast_to(h8[7:8, :], (8, w))

    hc[...] = lax.fori_loop(0, t_rows // 8, group, hc[...])
    y_ref[...] = (h_s[...] * _gelu_tanh(a2_ref[:, w:2 * w])).astype(BF16)


def _pair_block_diag(wb):
    z = jnp.zeros((128, 128), wb.dtype)
    return jnp.stack([jnp.block([[wb[2 * p], z], [z, wb[2 * p + 1]]]) for p in range(3)])


def _lru(a2, conv_w, conv_b, wa, ba, wi, bi, lam):
    s = a2.shape[0]
    w = LRU_WIDTH
    t = min(LRU_ROWS, s)
    wa2 = _pair_block_diag(wa.astype(BF16))
    wi2 = _pair_block_diag(wi.astype(BF16))
    vec = _resident((1, w))
    return pl.pallas_call(
        _lru_body,
        out_shape=jax.ShapeDtypeStruct((s, w), BF16),
        grid=(s // t,),
        in_specs=[pl.BlockSpec((t, 2 * w), lambda i: (i, 0)), _resident((CONV_WIDTH, w)), vec,
                  _resident(wa2.shape), vec, _resident(wi2.shape), vec, vec],
        out_specs=pl.BlockSpec((t, w), lambda i: (i, 0)),
        scratch_shapes=[pltpu.VMEM((t + 8, w), F32), pltpu.VMEM((t, w), F32), pltpu.VMEM((t, w), F32),
                        pltpu.VMEM((t, w), F32), pltpu.VMEM((8, w), F32)],
        compiler_params=_cparams(("arbitrary",)),
        name="lru",
    )(a2, conv_w, conv_b.reshape(1, w), wa2, ba.reshape(1, w), wi2, bi.reshape(1, w), lam.reshape(1, w))


CMP_HALF = CMP_STRIDE * NSA_KV_W


def _compress_body(x_ref, pos_ref, w1_ref, w2_ref, o_ref, *, transposed):
    hids = []
    for j in range(4):
        xa = (x_ref[:, CMP_HALF * j:CMP_HALF * (j + 1)] + pos_ref[:, :CMP_HALF]).astype(BF16)
        xb = (x_ref[:, CMP_HALF * (j + 1):CMP_HALF * (j + 2)] + pos_ref[:, CMP_HALF:]).astype(BF16)
        pre = _dot(xa, w1_ref[:CMP_HALF, :]) + _dot(xb, w1_ref[CMP_HALF:, :])
        hids.append(_gelu_tanh(pre).astype(BF16))
    hid = jnp.concatenate(hids, axis=0)
    if transposed:
        o_ref[...] = _dot_nt(w2_ref[...], hid).astype(BF16)
    else:
        o_ref[...] = _dot(hid, w2_ref[...]).astype(BF16)


def _compress(kv, pos, w1, w2, *, transposed):
    s = kv.shape[0]
    n_sel = s // SEL_BLOCK
    assert n_sel % CMP_TILE == 0
    rows = kv.reshape(n_sel, SEL_BLOCK * NSA_KV_W)
    nxt = jnp.concatenate([rows[1:, :CMP_HALF], jnp.zeros((1, CMP_HALF), kv.dtype)], axis=0)
    ext = jnp.concatenate([rows, nxt], axis=1)
    eye = jnp.eye(NSA_KV_HEADS, dtype=F32)
    w1e = jnp.einsum("ldn,hg->lhdgn", w1.reshape(CMP_BLOCK, HEAD_DIM, -1), eye)
    w1e = w1e.reshape(CMP_BLOCK * NSA_KV_W, -1).astype(BF16)
    hid = w1.shape[1]
    if transposed:
        w2e = jnp.einsum("nd,hg->gdhn", w2, eye).reshape(NSA_KV_W, NSA_KV_HEADS * hid).astype(BF16)
        out_shape = jax.ShapeDtypeStruct((NSA_KV_W, 4 * n_sel), BF16)
        out_spec = pl.BlockSpec((NSA_KV_W, 4 * CMP_TILE), lambda tt: (0, tt))
    else:
        w2p = jnp.pad(w2, ((0, 0), (0, LANES - HEAD_DIM)))
        w2e = jnp.einsum("nd,hg->hngd", w2p, eye).reshape(NSA_KV_HEADS * hid, NSA_KV_HEADS * LANES).astype(BF16)
        out_shape = jax.ShapeDtypeStruct((4 * n_sel, NSA_KV_HEADS * LANES), BF16)
        out_spec = pl.BlockSpec((4 * CMP_TILE, NSA_KV_HEADS * LANES), lambda tt: (tt, 0))
    pose = jnp.broadcast_to(pos[:, None, :], (CMP_BLOCK, NSA_KV_HEADS, HEAD_DIM)).reshape(1, -1)
    return pl.pallas_call(
        functools.partial(_compress_body, transposed=transposed),
        out_shape=out_shape,
        grid=(n_sel // CMP_TILE,),
        in_specs=[pl.BlockSpec((CMP_TILE, ext.shape[1]), lambda tt: (tt, 0)),
                  _resident(pose.shape), _resident(w1e.shape), _resident(w2e.shape)],
        out_specs=out_spec,
        compiler_params=_cparams(("parallel",)),
        name="compress_v" if transposed else "compress_k",
    )(ext, pose, w1e, w2e)


def _cmp_topk_tiles(qt_ref, kc_ref, vct_ref, oct_ref, bias_ref, *, n_tiles, n_sel, n_top):
    i = pl.program_id(0)
    n_rows = 4 * CMP_TILE * n_tiles
    n_blk = CMP_TILE * n_tiles
    lane_t = i * Q_BLOCK + lax.broadcasted_iota(jnp.int32, (1, Q_BLOCK), 1)
    r = lax.broadcasted_iota(jnp.int32, (n_rows, Q_BLOCK), 0)
    n_of_row = CMP_TILE * (r // (4 * CMP_TILE)) + r % CMP_TILE
    j_of_row = (r % (4 * CMP_TILE)) // CMP_TILE
    valid = SEL_BLOCK * n_of_row + CMP_STRIDE * j_of_row + (CMP_BLOCK - 1) <= lane_t
    has_valid = lane_t >= CMP_BLOCK - 1
    blk = lax.broadcasted_iota(jnp.int32, (n_blk, Q_BLOCK), 0)
    blk_f = blk.astype(F32)
    cur = lane_t // SEL_BLOCK
    forced = (blk == cur) | (blk == 0)
    future = blk > cur

    scores = []
    for k in range(NSA_KV_HEADS):
        kc = kc_ref[0:n_rows, LANES * k:LANES * k + HEAD_DIM]
        vct = jnp.concatenate([vct_ref[HEAD_DIM * k:HEAD_DIM * (k + 1), 0:n_rows],
                               jnp.ones((SUM_ROWS, n_rows), BF16)], axis=0)
        imp = jnp.zeros((n_blk, Q_BLOCK), F32)
        raw = [_dot(kc, qt_ref[HEAD_DIM * (NSA_GROUP * k + g):HEAD_DIM * (NSA_GROUP * k + g + 1), :])
               for g in range(NSA_GROUP)]
        for g in range(NSA_GROUP):
            h = NSA_GROUP * k + g
            s = jnp.where(valid, raw[g], NEG_INF)
            m = jnp.max(s, axis=0, keepdims=True)
            e = jnp.exp2(s - m)
            pv = _dot(vct, e.astype(BF16))
            den = jnp.maximum(pv[HEAD_DIM:HEAD_DIM + 1, :], 1e-30)
            inv = jnp.where(has_valid, 1.0 / den, 0.0)
            oct_ref[HEAD_DIM * h:HEAD_DIM * (h + 1), :] = pv[0:HEAD_DIM, :] * inv
            parts = []
            for tt in range(n_tiles):
                base = 4 * CMP_TILE * tt
                acc = e[base:base + CMP_TILE, :]
                for j in range(1, 4):
                    acc = acc + e[base + j * CMP_TILE:base + (j + 1) * CMP_TILE, :]
                parts.append(acc)
            imp = imp + jnp.concatenate(parts, axis=0) * inv
        scores.append(jnp.where(forced, -jnp.inf, jnp.where(future, -1.0, imp)))

    def pick_one(_, carry):
        out = []
        for v in carry:
            top = jnp.max(v, axis=0, keepdims=True)
            first = jnp.min(jnp.where(v == top, blk_f, float(n_blk)), axis=0, keepdims=True)
            out.append(jnp.where(blk_f == first, -jnp.inf, v))
        return tuple(out)

    assert n_top >= 2 and FORCE_SCORE > NSA_GROUP
    res = lax.fori_loop(0, n_top - 2, pick_one, tuple(scores))
    for k in range(NSA_KV_HEADS):
        bias_ref[k, 0:n_blk, :] = jnp.where(res[k] == -jnp.inf, 0.0, NEG_INF)
        if n_blk < n_sel:
            bias_ref[k, n_blk:n_sel, :] = jnp.full((n_sel - n_blk, Q_BLOCK), NEG_INF, F32)


def _cmp_topk_body(qt_ref, kc_ref, vct_ref, oct_ref, bias_ref, *, n_sel, n_top):
    total = n_sel // CMP_TILE
    need = jnp.minimum((2 * pl.program_id(0) + 1) // CMP_TILE + 1, total)
    for n_tiles in range(1, total + 1):
        pl.when(need == n_tiles)(functools.partial(
            _cmp_topk_tiles, qt_ref, kc_ref, vct_ref, oct_ref, bias_ref,
            n_tiles=n_tiles, n_sel=n_sel, n_top=n_top))


def _cmp_topk(qt, kc, vct):
    s = qt.shape[1]
    n_sel = s // SEL_BLOCK
    n_top = min(SEL_TOPN, n_sel)
    qspec = pl.BlockSpec((NSA_Q_W, Q_BLOCK), lambda i: (0, i))
    return pl.pallas_call(
        functools.partial(_cmp_topk_body, n_sel=n_sel, n_top=n_top),
        out_shape=[jax.ShapeDtypeStruct((NSA_Q_W, s), F32),
                   jax.ShapeDtypeStruct((NSA_KV_HEADS, n_sel, s), F32)],
        grid=(s // Q_BLOCK,),
        in_specs=[qspec, _resident(kc.shape), _resident(vct.shape)],
        out_specs=[qspec, pl.BlockSpec((NSA_KV_HEADS, n_sel, Q_BLOCK), lambda i: (0, 0, i))],
        compiler_params=_cparams(("parallel",)),
        name="cmp_topk",
    )(qt, kc, vct)


def _band_body(*refs, mode, k_off, v_off):
    if mode == "sel":
        qt_ref, k_ref, vt_ref, bias_ref, o_ref, acc_s, ml_s, s_s = refs
    else:
        qt_ref, k_ref, vt_ref, o_ref, acc_s, ml_s, s_s = refs
        bias_ref = None
    i = pl.program_id(0)
    wide = NSA_GROUP * BAND_Q
    blocks = KEY_CHUNK // SEL_BLOCK
    lane = lax.broadcasted_iota(jnp.int32, (1, wide), 1)
    t = i * BAND_Q + lane % BAND_Q
    krow = lax.broadcasted_iota(jnp.int32, (KEY_CHUNK, wide), 0)
    c_hi = (i * BAND_Q + (BAND_Q - 1)) // KEY_CHUNK + 1

    acc_s[...] = jnp.zeros_like(acc_s)
    ml_s[...] = jnp.full(ml_s.shape, NEG_INF, F32)
    ones = jnp.ones((SUM_ROWS, KEY_CHUNK), BF16)

    def scores(c, k):
        base = pl.multiple_of(c * KEY_CHUNK, KEY_CHUNK)
        q4 = jnp.concatenate(
            [qt_ref[HEAD_DIM * (NSA_GROUP * k + g):HEAD_DIM * (NSA_GROUP * k + g + 1), :]
             for g in range(NSA_GROUP)], axis=1)
        keys = k_ref[pl.ds(base, KEY_CHUNK), k_off + LANES * k:k_off + LANES * k + HEAD_DIM]
        s_s[k] = _dot(keys, q4)

    def softmax(c, k, masked):
        base = pl.multiple_of(c * KEY_CHUNK, KEY_CHUNK)
        s = s_s[k]
        if mode == "sel":
            rows = []
            for rblk in range(blocks):
                b = bias_ref[k, pl.ds(c * blocks + rblk, 1), :]
                b = jnp.concatenate([b] * NSA_GROUP, axis=1)
                rows.append(jnp.broadcast_to(b, (SEL_BLOCK, wide)))
            s = s + jnp.concatenate(rows, axis=0)
            if masked:
                s = jnp.where(base + krow <= t, s, NEG_INF)
        else:
            dist = t - (base + krow)
            s = jnp.where((dist >= 0) & (dist < WIN), s, NEG_INF)
        m_old = ml_s[k:k + 1, :]
        m_new = jnp.maximum(m_old, jnp.max(s, axis=0, keepdims=True))
        ml_s[k:k + 1, :] = m_new
        return jnp.exp2(m_old - m_new), jnp.exp2(s - m_new).astype(BF16)

    def accumulate(c, k, alpha, e):
        vt = jnp.concatenate([vt_ref[c, v_off + HEAD_DIM * k:v_off + HEAD_DIM * (k + 1), :], ones], axis=0)
        acc_s[k] = alpha * acc_s[k] + _dot(vt, e)

    def step(c, masked, prefetch):
        for k in range(NSA_KV_HEADS):
            alpha, e = softmax(c, k, masked)
            accumulate(c, k, alpha, e)
            if prefetch:
                scores(c + 1, k)

    if mode == "sel":
        c_lo = 0
    else:
        c_lo = jnp.maximum(i * BAND_Q - (WIN - 1), 0) // KEY_CHUNK
    for k in range(NSA_KV_HEADS):
        scores(c_lo, k)
    early_masked = mode == "win"
    n_early = c_hi - 1 - c_lo

    def group(p, carry):
        for j in range(BAND_UNROLL):
            step(c_lo + BAND_UNROLL * p + j, early_masked, True)
        return carry

    n_groups = n_early // BAND_UNROLL
    lax.fori_loop(0, n_groups, group, 0)
    rest_lo = c_lo + BAND_UNROLL * n_groups
    for j in range(BAND_UNROLL - 1):
        pl.when(rest_lo + j < c_hi - 1)(functools.partial(step, rest_lo + j, early_masked, True))
    step(c_hi - 1, True, False)

    for k in range(NSA_KV_HEADS):
        den = jnp.maximum(acc_s[k, HEAD_DIM:HEAD_DIM + 1, :], 1e-30)
        out = acc_s[k, 0:HEAD_DIM, :] * (1.0 / den)
        for g in range(NSA_GROUP):
            h = NSA_GROUP * k + g
            o_ref[HEAD_DIM * h:HEAD_DIM * (h + 1), :] = out[:, BAND_Q * g:BAND_Q * (g + 1)]


def _band(qt, ksw, vswt, sel, *, mode):
    s = qt.shape[1]
    assert (BAND_Q % KEY_CHUNK == 0 or KEY_CHUNK % BAND_Q == 0) and s % BAND_Q == 0
    qspec = pl.BlockSpec((NSA_Q_W, BAND_Q), lambda i: (0, i))
    k_off = 0 if mode == "sel" else NSA_KV_HEADS * LANES
    v_off = 0 if mode == "sel" else NSA_KV_W
    in_specs = [qspec, _resident(ksw.shape), _resident(vswt.shape)]
    args = [qt, ksw, vswt]
    if mode == "sel":
        in_specs.append(pl.BlockSpec((NSA_KV_HEADS, sel.shape[1], BAND_Q), lambda i: (0, 0, i)))
        args.append(sel)
    wide = NSA_GROUP * BAND_Q
    return pl.pallas_call(
        functools.partial(_band_body, mode=mode, k_off=k_off, v_off=v_off),
        out_shape=jax.ShapeDtypeStruct((NSA_Q_W, s), F32),
        grid=(s // BAND_Q,),
        in_specs=in_specs,
        out_specs=qspec,
        scratch_shapes=[pltpu.VMEM((NSA_KV_HEADS, HEAD_DIM + SUM_ROWS, wide), F32), pltpu.VMEM((8, wide), F32),
                        pltpu.VMEM((NSA_KV_HEADS, KEY_CHUNK, wide), F32)],
        compiler_params=_cparams(("parallel",)),
        name="nsa_" + mode,
    )(*args)


def _dil_body(q_ref, kp_ref, kc_ref, vp_ref, vc_ref, o_ref, lse_ref, *, span, subs):
    i = pl.program_id(1)
    rq = lax.broadcasted_iota(jnp.int32, (Q_BLOCK, 2 * Q_BLOCK), 0)
    ck = lax.broadcasted_iota(jnp.int32, (Q_BLOCK, 2 * Q_BLOCK), 1)
    delta = Q_BLOCK + rq - ck
    in_span = (delta >= 0) & (delta <= span)
    for sub in range(subs):
        rows = slice(Q_BLOCK * sub, Q_BLOCK * (sub + 1))
        valid = in_span & ((i * subs + sub - 1) * Q_BLOCK + ck >= 0)

        def pair(prev_ref, cur_ref, lanes):
            if sub == 0:
                return jnp.concatenate([prev_ref[:, lanes], cur_ref[rows, lanes]], axis=0)
            return cur_ref[Q_BLOCK * (sub - 1):Q_BLOCK * (sub + 1), lanes]

        raw = []
        for j in range(DIL_HEADS_PER_GROUP):
            q = q_ref[rows, LANES * j:LANES * j + HEAD_DIM]
            raw.append(_dot_nt(q, pair(kp_ref, kc_ref, slice(LANES * j, LANES * j + HEAD_DIM))))
        for j in range(DIL_HEADS_PER_GROUP):
            vv = pair(vp_ref, vc_ref, slice(LANES * j, LANES * (j + 1)))
            s = jnp.where(valid, raw[j], NEG_INF)
            m = jnp.max(s, axis=1, keepdims=True)
            e = jnp.where(valid, jnp.exp(s - m), 0.0)
            den = jnp.maximum(jnp.sum(e, axis=1, keepdims=True), 1e-30)
            p = e * (1.0 / den)
            o_ref[rows, LANES * j:LANES * (j + 1)] = _dot(p.astype(BF16), vv)
            lse_ref[rows, LANES * j:LANES * (j + 1)] = jnp.broadcast_to(m + jnp.log(den), (Q_BLOCK, LANES))


def _dilated_group(c, gi):
    window, dil = DIL_GROUPS[gi]
    span = window // dil
    assert span == Q_BLOCK
    view = c
    rows = view.shape[0]
    s = rows * dil
    subs = min(DIL_SUBS, rows // Q_BLOCK)
    big = subs * Q_BLOCK
    assert rows % big == 0
    spec = lambda tensor, prev: pl.BlockSpec(
        (Q_BLOCK, DIL_BLOCK_W) if prev else (big, DIL_BLOCK_W),
        lambda r, i: (jnp.maximum(i * subs - 1, 0) if prev else i, 3 * r + tensor))
    ospec = pl.BlockSpec((big, DIL_BLOCK_W), lambda r, i: (i, r))
    o, lse = pl.pallas_call(
        functools.partial(_dil_body, span=span, subs=subs),
        out_shape=[jax.ShapeDtypeStruct((rows, dil * DIL_BLOCK_W), F32)] * 2,
        grid=(dil, rows // big),
        in_specs=[spec(0, False), spec(1, True), spec(1, False), spec(2, True), spec(2, False)],
        out_specs=[ospec, ospec],
        compiler_params=_cparams(("parallel", "parallel")),
        name="dilated_%d" % gi,
    )(view, view, view, view, view)
    return o, lse


def _merge_body(x_ref, g_ref, ya_ref, oct_ref, ost_ref, owt_ref, glt_ref,
                o0_ref, l0_ref, o1_ref, l1_ref, o2_ref, l2_ref,
                wm_ref, wua_ref, wub_ref, wuc_ref, wo_ref, out_ref, nat_s):
    tm = x_ref.shape[0]

    def natural(view_ref, dil, slot):
        if dil == 1:
            return view_ref[...]
        for r in range(dil):
            for sl in range(DIL_BLOCK_W // LANES):
                col = r * DIL_BLOCK_W + LANES * sl
                nat_s[slot, sl, pl.ds(r, tm // dil, stride=dil), :] = view_ref[:, col:col + LANES]
        return jnp.concatenate([nat_s[slot, sl] for sl in range(DIL_BLOCK_W // LANES)], axis=1)

    dils = [dil for _, dil in DIL_GROUPS]
    o0, o1, o2 = (natural(r, dl, 2 * n) for n, (r, dl) in enumerate(zip((o0_ref, o1_ref, o2_ref), dils)))
    l0, l1, l2 = (natural(r, dl, 2 * n + 1) for n, (r, dl) in enumerate(zip((l0_ref, l1_ref, l2_ref), dils)))
    x = x_ref[...]
    h = _rms(x, g_ref[...]).astype(BF16)
    gates = jax.nn.sigmoid(_dot_nt(h, wm_ref[...]))
    y_a = _dot(ya_ref[...], wua_ref[...])

    bg = jax.nn.sigmoid(glt_ref[...])
    pieces = []
    for hd in range(NSA_HEADS):
        rows = slice(HEAD_DIM * hd, HEAD_DIM * (hd + 1))
        pieces.append(bg[3 * hd:3 * hd + 1, :] * oct_ref[rows, :]
                      + bg[3 * hd + 1:3 * hd + 2, :] * ost_ref[rows, :]
                      + bg[3 * hd + 2:3 * hd + 3, :] * owt_ref[rows, :])
    yb_t = jnp.concatenate(pieces, axis=0)
    y_b = _dot(yb_t.T.astype(BF16), wub_ref[...])

    mx = jnp.maximum(jnp.maximum(l0, l1), l2)
    e0, e1, e2 = jnp.exp(l0 - mx), jnp.exp(l1 - mx), jnp.exp(l2 - mx)
    inv = 1.0 / (e0 + e1 + e2)
    yc = (e0 * inv) * o0 + (e1 * inv) * o1 + (e2 * inv) * o2
    y_c = _dot(yc.astype(BF16), wuc_ref[...])

    d = x.shape[1]
    merged = gates[:, 0:d] * y_a + gates[:, d:2 * d] * y_b + gates[:, 2 * d:3 * d] * y_c
    out_ref[...] = x + _dot(merged.astype(BF16), wo_ref[...])


def _merge(x, g, ya, oct_, ost, owt, glt, dil, w_in, w_up_a, w_up_b, w_up_c, w_out):
    s, d = x.shape
    tm = min(MERGE_ROWS, s)
    w_m = w_in[IN_OFFSETS[12]:, :].astype(BF16)
    wuc = w_up_c.reshape(DIL_HEADS_PER_GROUP, HEAD_DIM, d)
    wuc = jnp.pad(wuc, ((0, 0), (0, LANES - HEAD_DIM), (0, 0))).reshape(DIL_BLOCK_W, d).astype(BF16)
    weights = [w_m, w_up_a.astype(BF16), w_up_b.astype(BF16), wuc, w_out.astype(BF16)]

    def rows(n):
        return pl.BlockSpec((tm, n), lambda i: (i, 0))

    def cols(n):
        return pl.BlockSpec((n, tm), lambda i: (0, i))

    dil_args, dil_specs = [], []
    for (o, lse), (_, dl) in zip(dil, DIL_GROUPS):
        dil_args += [o, lse]
        dil_specs += [pl.BlockSpec((tm // dl, dl * DIL_BLOCK_W), lambda i: (i, 0))] * 2
    return pl.pallas_call(
        _merge_body,
        out_shape=jax.ShapeDtypeStruct((s, d), F32),
        grid=(s // tm,),
        in_specs=[rows(d), _resident((1, d)), rows(LRU_WIDTH), cols(NSA_Q_W), cols(NSA_Q_W),
                  cols(NSA_Q_W), cols(GATE_ROWS)] + dil_specs + [_resident(w.shape) for w in weights],
        out_specs=rows(d),
        scratch_shapes=[pltpu.VMEM((2 * len(DIL_GROUPS), DIL_BLOCK_W // LANES, tm, LANES), F32)],
        compiler_params=_cparams(("parallel",)),
        name="merge",
    )(x, g.reshape(1, d), ya, oct_, ost, owt, glt, *dil_args, *weights)


def _mixer(x, mix_norm, w_in, conv_w, conv_b, lru_wa, lru_ba, lru_wi, lru_bi, lru_lambda,
           cmp_pos_k, cmp_pos_v, cmp_k_w1, cmp_k_w2, cmp_v_w1, cmp_v_w2,
           w_up_a, w_up_b, w_up_c, w_out):
    a2, qt, kvc, ksw, vswt, glt, *c = _proj(x, mix_norm, w_in)
    ya = _lru(a2, conv_w, conv_b, lru_wa, lru_ba, lru_wi, lru_bi, lru_lambda)
    kc = _compress(kvc[:, :NSA_KV_W], cmp_pos_k, cmp_k_w1, cmp_k_w2, transposed=False)
    vct = _compress(kvc[:, NSA_KV_W:], cmp_pos_v, cmp_v_w1, cmp_v_w2, transposed=True)
    oct_, sel = _cmp_topk(qt, kc, vct)
    ost = _band(qt, ksw, vswt, sel, mode="sel")
    owt = _band(qt, ksw, vswt, None, mode="win")
    dil = [_dilated_group(c[gi], gi) for gi in range(len(DIL_GROUPS))]
    return _merge(x, mix_norm, ya, oct_, ost, owt, glt, dil, w_in, w_up_a, w_up_b, w_up_c, w_out)


def kernel(x, ffn1_norm, ffn1_w1, ffn1_w3, ffn1_w2, mix_norm, w_in, conv_w, conv_b, lru_wa, lru_ba, lru_wi, lru_bi, lru_lambda, cmp_pos_k, cmp_pos_v, cmp_k_w1, cmp_k_w2, cmp_v_w1, cmp_v_w2, w_up_a, w_up_b, w_up_c, w_out, ffn2_norm, ffn2_w1, ffn2_w3, ffn2_w2, final_norm):
    batch, s, d = x.shape
    assert batch == 1 and d == D_MODEL
    depth = w_in.shape[0]
    w_in_t = jnp.swapaxes(w_in, 1, 2)
    y = x.reshape(s, d)
    for l in range(depth):
        y = _ffn(y, ffn1_norm[l], ffn1_w1[l], ffn1_w3[l], ffn1_w2[l], final_norm, final=False)
        y = _mixer(y, mix_norm[l], w_in_t[l], conv_w[l], conv_b[l], lru_wa[l], lru_ba[l], lru_wi[l],
                   lru_bi[l], lru_lambda[l], cmp_pos_k[l], cmp_pos_v[l], cmp_k_w1[l], cmp_k_w2[l],
                   cmp_v_w1[l], cmp_v_w2[l], w_up_a[l], w_up_b[l], w_up_c[l], w_out[l])
        y = _ffn(y, ffn2_norm[l], ffn2_w1[l], ffn2_w3[l], ffn2_w2[l], final_norm,
                 final=(l == depth - 1))
    return y.reshape(batch, s, d)
```
